```python
import math
import jax, jax.numpy as jnp
from jax import lax
import numpy as np

D_MODEL = 1024
BATCH = 8
SEQ = 4096
DEPTH = 1

HEAD_DIM = 64
N_HEADS = D_MODEL // 128
ATTN_WIDTH = N_HEADS * HEAD_DIM
Q_BLOCK = 128
CONV_WIDTH = D_MODEL // 2
CONV_K = 31
IN_COLS = 3 * ATTN_WIDTH + N_HEADS + 2 * CONV_WIDTH + 2 * D_MODEL
N_EXPERTS = 32
TOP_K = 4
EXPERT_FF = D_MODEL
SWIGLU_LIMIT = 7.0
SWIGLU_ALPHA = 1.702
GROUP_BLOCK = 128
PLE_DIM = 256

RMS_EPS = 1e-6
LN_EPS = 1e-5
NEG_INF = -1e30

kernel_name = "gated_fox_conformer_moe_block"


def rmsnorm(x, g):
    xf = x.astype(jnp.float32)
    y = xf * lax.rsqrt(jnp.mean(xf * xf, axis=-1, keepdims=True) + RMS_EPS)
    return (y * g.astype(jnp.float32)).astype(x.dtype)


def layernorm(x, g, b):
    xf = x.astype(jnp.float32)
    mu = jnp.mean(xf, axis=-1, keepdims=True)
    var = jnp.mean(jnp.square(xf - mu), axis=-1, keepdims=True)
    y = (xf - mu) * lax.rsqrt(var + LN_EPS)
    return (y * g.astype(jnp.float32) + b.astype(jnp.float32)).astype(x.dtype)


def fox_attention(q, k, v, log_f):
    B, S, H, dh = q.shape
    n_blk = S // Q_BLOCK
    scale = 1.0 / math.sqrt(dh)
    c = jnp.cumsum(log_f, axis=1).transpose(0, 2, 1)
    qh = q.transpose(0, 2, 1, 3)
    kh = k.transpose(0, 2, 1, 3)
    vh = v.transpose(0, 2, 1, 3)
    q_blocks = qh.reshape(B, H, n_blk, Q_BLOCK, dh).transpose(2, 0, 1, 3, 4)
    c_blocks = c.reshape(B, H, n_blk, Q_BLOCK).transpose(2, 0, 1, 3)
    key_pos = jnp.arange(S)

    def one_block(args):
        qi, cqi, bi = args
        s = jnp.einsum('bhqd,bhkd->bhqk', qi, kh).astype(jnp.float32) * scale
        s = s + cqi[..., :, None] - c[:, :, None, :]
        q_pos = bi * Q_BLOCK + jnp.arange(Q_BLOCK)
        s = jnp.where(key_pos[None, :] <= q_pos[:, None], s, NEG_INF)
        pr = jax.nn.softmax(s, axis=-1).astype(vh.dtype)
        return jnp.einsum('bhqk,bhkd->bhqd', pr, vh)

    o = lax.map(one_block, (q_blocks, c_blocks, jnp.arange(n_blk)))
    return o.transpose(1, 0, 3, 2, 4).reshape(B, S, H * dh)


def conformer_conv(u, conv_w, conv_b, ln_g, ln_b, w_out):
    a, gate = jnp.split(u, 2, axis=-1)
    z = a * jax.nn.sigmoid(gate)
    z = lax.conv_general_dilated(
        z, conv_w[:, None, :], window_strides=(1,), padding=[(CONV_K - 1, 0)],
        dimension_numbers=('NWC', 'WIO', 'NWC'), feature_group_count=CONV_WIDTH) + conv_b
    z = jax.nn.silu(layernorm(z, ln_g, ln_b))
    return z @ w_out


def moe_ffn(xn, w_router, b_router, w_gate, b_gate, w_up, b_up, w_down, b_down):
    Bsz, S, D = xn.shape
    T = Bsz * S
    xt = xn.reshape(T, D)
    logits = (xt @ w_router + b_router).astype(jnp.float32)
    top_val, top_idx = lax.top_k(logits, TOP_K)
    gates = jax.nn.softmax(top_val, axis=-1)

    n_assign = T * TOP_K
    e_flat = top_idx.reshape(-1)
    order = jnp.argsort(e_flat, stable=True)
    e_sorted = e_flat[order]
    tok_sorted = order // TOP_K
    counts = jnp.bincount(e_flat, length=N_EXPERTS)
    padded = ((counts + GROUP_BLOCK - 1) // GROUP_BLOCK) * GROUP_BLOCK
    start = jnp.cumsum(counts) - counts
    pad_end = jnp.cumsum(padded)
    pad_start = pad_end - padded
    dest = pad_start[e_sorted] + (jnp.arange(n_assign) - start[e_sorted])

    n_blocks = -(-n_assign // GROUP_BLOCK) + N_EXPERTS
    n_rows = n_blocks * GROUP_BLOCK
    x_pad = jnp.zeros((n_rows, D), xt.dtype).at[dest].set(xt[tok_sorted])
    block_expert = jnp.clip(
        jnp.searchsorted(pad_end, jnp.arange(n_blocks) * GROUP_BLOCK, side='right'),
        0, N_EXPERTS - 1)

    def expert_block(args):
        xb, e = args
        g = xb @ w_gate[e] + b_gate[e]
        u = xb @ w_up[e] + b_up[e]
        g = jnp.minimum(g, SWIGLU_LIMIT)
        u = jnp.clip(u, -SWIGLU_LIMIT, SWIGLU_LIMIT)
        glu = g * jax.nn.sigmoid(SWIGLU_ALPHA * g)
        return ((u + 1.0) * glu) @ w_down[e] + b_down[e]

    y_pad = lax.map(expert_block, (x_pad.reshape(n_blocks, GROUP_BLOCK, D), block_expert))
    y_sorted = y_pad.reshape(n_rows, D)[dest]
    w_sorted = gates.reshape(-1)[order].astype(y_sorted.dtype)
    out = jax.ops.segment_sum(y_sorted * w_sorted[:, None], tok_sorted, num_segments=T)
    return out.reshape(Bsz, S, D)


def setup_inputs(seed: int = 0) -> dict:
    key = jax.random.key(seed)
    ks = jax.random.split(key, 32)
    L, D, E, F = DEPTH, D_MODEL, N_EXPERTS, EXPERT_FF
    f32 = jnp.float32

    def nrm(k, shape, fan_in):
        return jax.random.normal(k, shape, f32) * (fan_in ** -0.5)

    def gain(k, shape):
        return 1.0 + 0.05 * jax.random.normal(k, shape, f32)

    def small(k, shape):
        return 0.02 * jax.random.normal(k, shape, f32)

    return {
        "x": jax.random.normal(ks[0], (BATCH, SEQ, D), f32),
        "p": jax.random.normal(ks[1], (DEPTH, BATCH, SEQ, PLE_DIM), f32),
        "norm_mix": gain(ks[2], (L, D)),
        "w_in": nrm(ks[3], (L, D, IN_COLS), D),
        "b_forget": jax.random.uniform(ks[4], (L, N_HEADS), f32, 1.0, 6.0),
        "w_attn_out": nrm(ks[5], (L, ATTN_WIDTH, D), ATTN_WIDTH),
        "conv_w": nrm(ks[6], (L, CONV_K, CONV_WIDTH), CONV_K),
        "conv_b": small(ks[7], (L, CONV_WIDTH)),
        "conv_ln_g": gain(ks[8], (L, CONV_WIDTH)),
        "conv_ln_b": small(ks[9], (L, CONV_WIDTH)),
        "w_conv_out": nrm(ks[10], (L, CONV_WIDTH, D), CONV_WIDTH),
        "w_o": nrm(ks[11], (L, D, D), D),
        "norm_ffn": gain(ks[12], (L, D)),
        "w_router": nrm(ks[13], (L, D, E), D),
        "b_router": 0.01 * jax.random.normal(ks[14], (L, E), f32),
        "w_gate": nrm(ks[15], (L, E, D, F), D),
        "b_gate": small(ks[16], (L, E, F)),
        "w_up": nrm(ks[17], (L, E, D, F), D),
        "b_up": small(ks[18], (L, E, F)),
        "w_down": nrm(ks[19], (L, E, F, D), F),
        "b_down": small(ks[20], (L, E, D)),
        "norm_ple": gain(ks[21], (L, D)),
        "w_ple_gate": nrm(ks[22], (L, D, D), D),
        "w_ple_proj": nrm(ks[23], (L, PLE_DIM, D), PLE_DIM),
        "norm_final": gain(ks[24], (D,)),
    }


def reference(x, p, norm_mix, w_in, b_forget, w_attn_out, conv_w, conv_b, conv_ln_g, conv_ln_b,
              w_conv_out, w_o, norm_ffn, w_router, b_router, w_gate, b_gate, w_up, b_up,
              w_down, b_down, norm_ple, w_ple_gate, w_ple_proj, norm_final):
    B, S, D = x.shape
    split_at = np.cumsum([ATTN_WIDTH, ATTN_WIDTH, ATTN_WIDTH, N_HEADS, 2 * CONV_WIDTH, D_MODEL])
    h = x
    for i in range(DEPTH):
        xn = rmsnorm(h, norm_mix[i])
        proj = xn @ w_in[i]
        q, k, v, f_logit, conv_in, g_attn, g_conv = jnp.split(proj, list(split_at), axis=-1)
        log_f = jax.nn.log_sigmoid((f_logit + b_forget[i]).astype(jnp.float32))
        attn = fox_attention(q.reshape(B, S, N_HEADS, HEAD_DIM),
                             k.reshape(B, S, N_HEADS, HEAD_DIM),
                             v.reshape(B, S, N_HEADS, HEAD_DIM), log_f)
        branch_a = attn @ w_attn_out[i]
        branch_c = conformer_conv(conv_in, conv_w[i], conv_b[i], conv_ln_g[i], conv_ln_b[i],
                                  w_conv_out[i])
        merged = jax.nn.sigmoid(g_attn) * branch_a + jax.nn.sigmoid(g_conv) * branch_c
        h = h + merged @ w_o[i]
        h = h + moe_ffn(rmsnorm(h, norm_ffn[i]), w_router[i], b_router[i], w_gate[i], b_gate[i],
                        w_up[i], b_up[i], w_down[i], b_down[i])
        pg = jax.nn.sigmoid(rmsnorm(h, norm_ple[i]) @ w_ple_gate[i])
        h = h + pg * (p[i] @ w_ple_proj[i])
    return rmsnorm(h, norm_final)
```

```python
import functools

import jax
import jax.numpy as jnp
import numpy as np
from jax import lax
from jax.experimental import pallas as pl
from jax.experimental.pallas import tpu as pltpu

F32 = jnp.float32
BF16 = jnp.bfloat16

D_MODEL = 1024
HEAD_DIM = 64
N_HEADS = 8
ATTN_WIDTH = N_HEADS * HEAD_DIM
CONV_WIDTH = 512
CONV_K = 31
N_EXPERTS = 32
TOP_K = 4
SWIGLU_LIMIT = 7.0
SWIGLU_ALPHA = 1.702
PLE_DIM = 256
RMS_EPS = 1e-6
LN_EPS = 1e-5
NEG_INF = -1e30

LANES = 128
VMEM_LIMIT = 56 * 1024 * 1024

ROW_TILE = 512
ATTN_TQ = 256
CUM_CHUNK = 256
CONV_HALO = 32
MOE_BLK = 256
DISPATCH_TILE = 256
COMBINE_TILE = 128
N_CSPLIT = 3


def _cp(sem):
    return pltpu.CompilerParams(dimension_semantics=sem, vmem_limit_bytes=VMEM_LIMIT)


def _dot(a, b):
    return jnp.dot(a, b, preferred_element_type=F32)


def _sigmoid(x):
    return 1.0 / (1.0 + jnp.exp(-x))


def _rms(x, g):
    return x * lax.rsqrt(jnp.mean(x * x, axis=-1, keepdims=True) + RMS_EPS) * g


_C_QKV = 3 * ATTN_WIDTH
_C_CONV = _C_QKV + 2 * CONV_WIDTH
_C_GA = _C_CONV + D_MODEL
_C_GC = _C_GA + D_MODEL
_C_END = _C_GC + LANES


def _inproj_kernel(x_ref, g_ref, w_ref, bf_ref, q_ref, k_ref, v_ref, lf_ref, z_ref, sga_ref, sgc_ref):
    xn = _rms(x_ref[...], g_ref[...]).astype(BF16)
    qkv = _dot(xn, w_ref[:, 0:_C_QKV])
    q_ref[...] = (qkv[:, 0:ATTN_WIDTH] * (HEAD_DIM ** -0.5)).astype(BF16)
    k_ref[...] = qkv[:, ATTN_WIDTH:2 * ATTN_WIDTH].astype(BF16)
    v_ref[...] = qkv[:, 2 * ATTN_WIDTH:3 * ATTN_WIDTH].astype(BF16)
    cv = _dot(xn, w_ref[:, _C_QKV:_C_CONV])
    z_ref[...] = cv[:, 0:CONV_WIDTH] * _sigmoid(cv[:, CONV_WIDTH:])
    sga_ref[...] = _sigmoid(_dot(xn, w_ref[:, _C_CONV:_C_GA])).astype(BF16)
    sgc_ref[...] = _sigmoid(_dot(xn, w_ref[:, _C_GA:_C_GC])).astype(BF16)
    f = _dot(xn, w_ref[:, _C_GC:_C_END]) + bf_ref[...]
    lf_ref[...] = jnp.minimum(f, 0.0) - jnp.log(1.0 + jnp.exp(-jnp.abs(f)))


def _inproj(x2, g, w, bf):
    T = x2.shape[0]
    tm = ROW_TILE
    row = lambda n: pl.BlockSpec((tm, n), lambda i: (i, 0))
    full = lambda a: pl.BlockSpec(a.shape, lambda i: (0,) * a.ndim)
    return pl.pallas_call(
        _inproj_kernel,
        grid=(T // tm,),
        in_specs=[row(D_MODEL), full(g), full(w), full(bf)],
        out_specs=[row(ATTN_WIDTH), row(ATTN_WIDTH), row(ATTN_WIDTH), row(LANES), row(CONV_WIDTH),
                   row(D_MODEL), row(D_MODEL)],
        out_shape=[jax.ShapeDtypeStruct((T, ATTN_WIDTH), BF16)] * 3
        + [jax.ShapeDtypeStruct((T, LANES), F32), jax.ShapeDtypeStruct((T, CONV_WIDTH), F32),
           jax.ShapeDtypeStruct((T, D_MODEL), BF16), jax.ShapeDtypeStruct((T, D_MODEL), BF16)],
        compiler_params=_cp(("parallel",)),
        name="inproj",
    )(x2, g, w, bf)


def _split3(x):
    hi = x.astype(BF16)
    r1 = x - hi.astype(F32)
    mid = r1.astype(BF16)
    lo = (r1 - mid.astype(F32)).astype(BF16)
    return hi, mid, lo


def _cumsum_kernel(lf_ref, o_ref):
    S = lf_ref.shape[1]
    n = CUM_CHUNK
    r = lax.broadcasted_iota(jnp.int32, (n, n), 0)
    c = lax.broadcasted_iota(jnp.int32, (n, n), 1)
    tri = jnp.where(c <= r, 1.0, 0.0).astype(BF16)
    group = lax.broadcasted_iota(jnp.int32, (n, LANES), 1) // N_HEADS

    def body(i, carry):
        r0 = pl.multiple_of(i * n, n)
        hi, mid, lo = _split3(lf_ref[0, pl.ds(r0, n), :])
        cs = _dot(tri, hi) + _dot(tri, mid) + _dot(tri, lo) + carry
        chi, cmid, clo = (a.astype(F32) for a in _split3(cs))
        out = jnp.where(group == 0, chi, jnp.where(group == 1, cmid, jnp.where(
            group == 2, clo, jnp.where(group == 3, -chi, jnp.where(group == 4, -cmid, -clo)))))
        o_ref[0, pl.ds(r0, n), :] = out.astype(BF16)
        return cs[n - 1:n, :]

    lax.fori_loop(0, S // n, body, jnp.zeros((1, LANES), F32))


def _cumsum(lf):
    B, S, _ = lf.shape
    spec = pl.BlockSpec((1, S, LANES), lambda b: (b, 0, 0))
    return pl.pallas_call(
        _cumsum_kernel, grid=(B,), in_specs=[spec], out_specs=spec,
        out_shape=jax.ShapeDtypeStruct((B, S, LANES), BF16),
        compiler_params=_cp(("parallel",)), name="logf_cumsum",
    )(lf)


def _attn_kernel(qa_ref, ka_ref, v_ref, o_ref):
    i = pl.program_id(2)
    t = ATTN_TQ
    dn = (((1,), (1,)), ((), ()))
    outs = []
    for hh in range(2):
        qa = qa_ref[0, hh]

        def tile(j, carry, masked):
            m, l, acc = carry
            r0 = pl.multiple_of(j * t, t)
            s = lax.dot_general(qa, ka_ref[0, hh, pl.ds(r0, t), :], dn, preferred_element_type=F32)
            if masked:
                row = lax.broadcasted_iota(jnp.int32, (t, t), 0)
                col = lax.broadcasted_iota(jnp.int32, (t, t), 1)
                s = jnp.where(col <= row, s, NEG_INF)
            m_new = jnp.maximum(m, jnp.max(s, axis=-1, keepdims=True))
            p = jnp.exp(s - m_new)
            alpha = jnp.exp(m - m_new)
            l = alpha * l + jnp.sum(p, axis=-1, keepdims=True)
            acc = alpha * acc + _dot(p.astype(BF16), v_ref[0, hh, pl.ds(r0, t), :])
            return m_new, l, acc

        init = (jnp.full((t, 1), NEG_INF, F32), jnp.zeros((t, 1), F32), jnp.zeros((t, HEAD_DIM), F32))
        carry = lax.fori_loop(0, i, lambda j, c: tile(j, c, False), init)
        m, l, acc = tile(i, carry, True)
        outs.append(acc / l)
    o_ref[0] = jnp.concatenate(outs, axis=-1).astype(BF16)


def _attention(qa, ka, vh):
    B, H, S, _ = qa.shape
    t = ATTN_TQ
    return pl.pallas_call(
        _attn_kernel,
        grid=(B, H // 2, S // t),
        in_specs=[pl.BlockSpec((1, 2, t, LANES), lambda b, h, i: (b, h, i, 0)),
                  pl.BlockSpec((1, 2, S, LANES), lambda b, h, i: (b, h, 0, 0)),
                  pl.BlockSpec((1, 2, S, HEAD_DIM), lambda b, h, i: (b, h, 0, 0))],
        out_specs=pl.BlockSpec((1, t, 2 * HEAD_DIM), lambda b, h, i: (b, i, h)),
        out_shape=jax.ShapeDtypeStruct((B, S, ATTN_WIDTH), BF16),
        compiler_params=_cp(("parallel", "parallel", "arbitrary")),
        name="fox_attention",
    )(qa, ka, vh)


def _mix_kernel(z_ref, zprev_ref, cw_ref, cb_ref, lng_ref, lnb_ref, attn_ref, sga_ref, sgc_ref, x_ref,
                wa_ref, wc_ref, wo_ref, nf_ref, wr_ref, br_ref, h_ref, xn_ref, lg_ref, zs_ref, cv_ref):
    ts = z_ref.shape[1]
    i = pl.program_id(1)
    halo = zprev_ref[0]
    zs_ref[0:CONV_HALO, :] = jnp.where(i > 0, halo, 0.0)
    zs_ref[CONV_HALO:CONV_HALO + ts, :] = z_ref[0]
    base = CONV_HALO - (CONV_K - 1)
    rc = 128
    for c0 in range(0, CONV_WIDTH, LANES):
        for r0 in range(0, ts, rc):
            acc = jnp.zeros((rc, LANES), F32)
            for j in range(CONV_K):
                acc = acc + cw_ref[j:j + 1, c0:c0 + LANES] * zs_ref[r0 + base + j:r0 + base + j + rc, c0:c0 + LANES]
            cv_ref[r0:r0 + rc, c0:c0 + LANES] = acc
    cv = cv_ref[...] + cb_ref[...]
    mu = jnp.mean(cv, axis=-1, keepdims=True)
    d = cv - mu
    var = jnp.mean(d * d, axis=-1, keepdims=True)
    y = d * lax.rsqrt(var + LN_EPS) * lng_ref[...] + lnb_ref[...]
    act = (y * _sigmoid(y)).astype(BF16)
    branch_c = _dot(act, wc_ref[...])
    branch_a = _dot(attn_ref[0], wa_ref[...])
    merged = sga_ref[0].astype(F32) * branch_a + sgc_ref[0].astype(F32) * branch_c
    h = x_ref[0] + _dot(merged.astype(BF16), wo_ref[...])
    h_ref[0] = h
    xn = _rms(h, nf_ref[...])
    xn_ref[0] = xn
    lg_ref[0] = _dot(xn.astype(BF16), wr_ref[...]) + br_ref[...]


def _mix(z, cw, cb, lng, lnb, attn, sga, sgc, x, wa, wc, wo, nf, wr, br):
    B, S, _ = x.shape
    ts = ROW_TILE
    hpb = ts // CONV_HALO
    row = lambda n: pl.BlockSpec((1, ts, n), lambda b, i: (b, i, 0))
    full = lambda a: pl.BlockSpec(a.shape, lambda b, i: (0,) * a.ndim)
    prev = pl.BlockSpec((1, CONV_HALO, CONV_WIDTH), lambda b, i: (b, jnp.maximum(i * hpb - 1, 0), 0))
    return pl.pallas_call(
        _mix_kernel,
        grid=(B, S // ts),
        in_specs=[row(CONV_WIDTH), prev, full(cw), full(cb), full(lng), full(lnb), row(ATTN_WIDTH),
                  row(D_MODEL), row(D_MODEL), row(D_MODEL), full(wa), full(wc), full(wo), full(nf),
                  full(wr), full(br)],
        out_specs=[row(D_MODEL), row(D_MODEL), row(LANES)],
        out_shape=[jax.ShapeDtypeStruct((B, S, D_MODEL), F32), jax.ShapeDtypeStruct((B, S, D_MODEL), F32),
                   jax.ShapeDtypeStruct((B, S, LANES), F32)],
        scratch_shapes=[pltpu.VMEM((CONV_HALO + ts, CONV_WIDTH), F32), pltpu.VMEM((ts, CONV_WIDTH), F32)],
        compiler_params=_cp(("parallel", "parallel")),
        name="mix_merge",
    )(z, z, cw, cb, lng, lnb, attn, sga, sgc, x, wa, wc, wo, nf, wr, br)


def _route_kernel(lg_ref, o_ref, cnt_ref, carry_ref):
    tr = lg_ref.shape[0]

    @pl.when(pl.program_id(0) == 0)
    def _():
        carry_ref[...] = jnp.zeros_like(carry_ref)

    lane = lax.broadcasted_iota(jnp.int32, (tr, LANES), 1).astype(F32)
    vals = lg_ref[...]
    top_v, top_i, hots = [], [], []
    for _ in range(TOP_K):
        m = jnp.max(vals, axis=-1, keepdims=True)
        idx = jnp.min(jnp.where(vals == m, lane, float(LANES)), axis=-1, keepdims=True)
        hot = lane == idx
        vals = jnp.where(hot, -jnp.inf, vals)
        top_v.append(m)
        top_i.append(idx)
        hots.append(hot)
    ex = [jnp.exp(v - top_v[0]) for v in top_v]
    den = ex[0] + ex[1] + ex[2] + ex[3]
    onehot = jnp.zeros((tr, LANES), F32)
    for hot in hots:
        onehot = onehot + jnp.where(hot, 1.0, 0.0)
    r = lax.broadcasted_iota(jnp.int32, (tr, tr), 0)
    c = lax.broadcasted_iota(jnp.int32, (tr, tr), 1)
    strict = jnp.where(c < r, 1.0, 0.0).astype(BF16)
    before = _dot(strict, onehot.astype(BF16)) + carry_ref[...]
    out = jnp.zeros((tr, LANES), F32)
    for k in range(TOP_K):
        rank = jnp.sum(jnp.where(hots[k], before, 0.0), axis=-1, keepdims=True)
        out = jnp.where(lane == k, top_i[k], out)
        out = jnp.where(lane == TOP_K + k, rank, out)
        out = jnp.where(lane == 2 * TOP_K + k, ex[k] / den, out)
    o_ref[...] = out
    total = carry_ref[...] + jnp.sum(onehot, axis=0, keepdims=True)
    carry_ref[...] = total
    cnt_ref[...] = total


def _route(lg):
    T = lg.shape[0]
    tr = ROW_TILE
    return pl.pallas_call(
        _route_kernel,
        grid=(T // tr,),
        in_specs=[pl.BlockSpec((tr, LANES), lambda i: (i, 0))],
        out_specs=[pl.BlockSpec((tr, LANES), lambda i: (i, 0)), pl.BlockSpec((1, LANES), lambda i: (0, 0))],
        out_shape=[jax.ShapeDtypeStruct((T, LANES), F32), jax.ShapeDtypeStruct((1, LANES), F32)],
        scratch_shapes=[pltpu.VMEM((1, LANES), F32)],
        compiler_params=_cp(("arbitrary",)),
        name="route_topk",
    )(lg)


def _dest_kernel(rt_ref, ps_ref, o_ref):
    tr = rt_ref.shape[0]
    lane = lax.broadcasted_iota(jnp.int32, (tr, LANES), 1).astype(F32)
    rt = rt_ref[...]
    out = jnp.zeros((tr, LANES), F32)
    for k in range(TOP_K):
        start = jnp.sum(jnp.where(lane == rt[:, k:k + 1], ps_ref[...], 0.0), axis=-1, keepdims=True)
        out = jnp.where(lane == k, start + rt[:, TOP_K + k:TOP_K + k + 1], out)
    o_ref[...] = out.astype(jnp.int32)


def _dest(rt, pad_start):
    T = rt.shape[0]
    tr = ROW_TILE
    return pl.pallas_call(
        _dest_kernel,
        grid=(T // tr,),
        in_specs=[pl.BlockSpec((tr, LANES), lambda i: (i, 0)), pl.BlockSpec((1, LANES), lambda i: (0, 0))],
        out_specs=pl.BlockSpec((tr, LANES), lambda i: (i, 0)),
        out_shape=jax.ShapeDtypeStruct((T, LANES), jnp.int32),
        compiler_params=_cp(("parallel",)),
        name="route_dest",
    )(rt, pad_start)


def _row_copy(src_ref, s, dst_ref, d, sem):
    return pltpu.make_async_copy(src_ref.at[pl.ds(s, 1)], dst_ref.at[pl.ds(d, 1)], sem)


def _dispatch_kernel(tail_ref, nfill_ref, dest_ref, x_ref, o_ref, zero_ref, sem):
    tb = x_ref.shape[0]
    i = pl.program_id(0)

    @pl.when(i == 0)
    def _():
        zero_ref[...] = jnp.zeros_like(zero_ref)

        def fill(n):
            return pltpu.make_async_copy(zero_ref, o_ref.at[pl.ds(pl.multiple_of(tail_ref[n], MOE_BLK), MOE_BLK)], sem)

        lax.fori_loop(0, nfill_ref[0], lambda n, c: (fill(n).start(), c)[1], 0)
        lax.fori_loop(0, nfill_ref[0], lambda n, c: (fill(n).wait(), c)[1], 0)

    def start(r, c):
        for k in range(TOP_K):
            _row_copy(x_ref, r, o_ref, dest_ref[0, 0, r * TOP_K + k], sem).start()
        return c

    def wait(r, c):
        for k in range(TOP_K):
            _row_copy(x_ref, r, o_ref, dest_ref[0, 0, r * TOP_K + k], sem).wait()
        return c

    lax.fori_loop(0, tb, start, 0)
    lax.fori_loop(0, tb, wait, 0)


def _dispatch(tail_rows, nfill, dest_tiles, xn, n_rows):
    T = xn.shape[0]
    tb = DISPATCH_TILE
    return pl.pallas_call(
        _dispatch_kernel,
        grid_spec=pltpu.PrefetchScalarGridSpec(
            num_scalar_prefetch=2,
            grid=(T // tb,),
            in_specs=[pl.BlockSpec((1, 1, tb * TOP_K), lambda i, *_: (i, 0, 0), memory_space=pltpu.SMEM),
                      pl.BlockSpec((tb, D_MODEL), lambda i, *_: (i, 0))],
            out_specs=pl.BlockSpec(memory_space=pl.ANY),
            scratch_shapes=[pltpu.VMEM((MOE_BLK, D_MODEL), F32), pltpu.SemaphoreType.DMA(())],
        ),
        out_shape=jax.ShapeDtypeStruct((n_rows, D_MODEL), F32),
        compiler_params=_cp(("arbitrary",)),
        name="moe_dispatch",
    )(tail_rows, nfill, dest_tiles, xn)


def _expert_kernel(be_ref, nused_ref, x_ref, wg_ref, bg_ref, wu_ref, bu_ref, wd_ref, bd_ref, y_ref):
    i = pl.program_id(0)

    @pl.when(i < nused_ref[0])
    def _():
        xb = x_ref[...].astype(BF16)
        g = _dot(xb, wg_ref[0]) + bg_ref[0]
        u = _dot(xb, wu_ref[0]) + bu_ref[0]
        g = jnp.minimum(g, SWIGLU_LIMIT)
        u = jnp.clip(u, -SWIGLU_LIMIT, SWIGLU_LIMIT)
        glu = g * _sigmoid(SWIGLU_ALPHA * g)
        y_ref[...] = _dot(((u + 1.0) * glu).astype(BF16), wd_ref[0]) + bd_ref[0]

    @pl.when(i >= nused_ref[0])
    def _():
        y_ref[...] = jnp.zeros_like(y_ref)


def _experts(block_expert, nused, x_pad, wg, bg, wu, bu, wd, bd):
    n_rows = x_pad.shape[0]
    wspec = pl.BlockSpec((1, D_MODEL, D_MODEL), lambda i, be, nu: (be[i], 0, 0))
    bspec = pl.BlockSpec((1, 1, D_MODEL), lambda i, be, nu: (be[i], 0, 0))
    xspec = pl.BlockSpec((MOE_BLK, D_MODEL), lambda i, be, nu: (i, 0))
    return pl.pallas_call(
        _expert_kernel,
        grid_spec=pltpu.PrefetchScalarGridSpec(
            num_scalar_prefetch=2,
            grid=(n_rows // MOE_BLK,),
            in_specs=[xspec, wspec, bspec, wspec, bspec, wspec, bspec],
            out_specs=xspec,
        ),
        out_shape=jax.ShapeDtypeStruct((n_rows, D_MODEL), F32),
        compiler_params=_cp(("arbitrary",)),
        name="moe_experts",
    )(block_expert, nused, x_pad, wg, bg, wu, bu, wd, bd)


def _tail_kernel(dest_ref, y_ref, rt_ref, h_ref, p_ref, npl_ref, wpg_ref, wpp_ref, nfin_ref, o_ref, buf_ref, sem):
    tc = h_ref.shape[0]

    def start(r, c):
        for k in range(TOP_K):
            _row_copy(y_ref, dest_ref[0, 0, r * TOP_K + k], buf_ref.at[k], r, sem).start()
        return c

    def wait(r, c):
        for k in range(TOP_K):
            _row_copy(y_ref, dest_ref[0, 0, r * TOP_K + k], buf_ref.at[k], r, sem).wait()
        return c

    lax.fori_loop(0, tc, start, 0)
    lax.fori_loop(0, tc, wait, 0)
    rt = rt_ref[...]
    h = h_ref[...]
    for k in range(TOP_K):
        h = h + buf_ref[k] * rt[:, 2 * TOP_K + k:2 * TOP_K + k + 1]
    pg = _sigmoid(_dot(_rms(h, npl_ref[...]).astype(BF16), wpg_ref[...]))
    h = h + pg * _dot(p_ref[...].astype(BF16), wpp_ref[...])
    o_ref[...] = _rms(h, nfin_ref[...])


def _tail(dest_tiles, y_pad, rt, h1, p2, npl, wpg, wpp, nfin):
    T = h1.shape[0]
    tc = COMBINE_TILE
    row = lambda n: pl.BlockSpec((tc, n), lambda i: (i, 0))
    full = lambda a: pl.BlockSpec(a.shape, lambda i: (0,) * a.ndim)
    return pl.pallas_call(
        _tail_kernel,
        grid=(T // tc,),
        in_specs=[pl.BlockSpec((1, 1, tc * TOP_K), lambda i: (i, 0, 0), memory_space=pltpu.SMEM),
                  pl.BlockSpec(memory_space=pl.ANY), row(LANES), row(D_MODEL), row(PLE_DIM),
                  full(npl), full(wpg), full(wpp), full(nfin)],
        out_specs=row(D_MODEL),
        out_shape=jax.ShapeDtypeStruct((T, D_MODEL), F32),
        scratch_shapes=[pltpu.VMEM((TOP_K, tc, D_MODEL), F32), pltpu.SemaphoreType.DMA(())],
        compiler_params=_cp(("arbitrary",)),
        name="moe_combine_tail",
    )(dest_tiles, y_pad, rt, h1, p2, npl, wpg, wpp, nfin)


def _layer(h, p_l, norm_mix, w_in, b_forget, w_attn_out, conv_w, conv_b, conv_ln_g, conv_ln_b, w_conv_out, w_o,
           norm_ffn, w_router, b_router, w_gate, b_gate, w_up, b_up, w_down, b_down, norm_ple, w_ple_gate,
           w_ple_proj):
    B, S, D = h.shape
    T = B * S
    H = N_HEADS
    row2 = lambda a: a.reshape(1, -1)

    o_f = 3 * ATTN_WIDTH
    o_conv = o_f + N_HEADS
    o_ga = o_conv + 2 * CONV_WIDTH
    reps = LANES // N_HEADS
    w_f = jnp.tile(w_in[:, o_f:o_conv], (1, reps))
    w_all = jnp.concatenate([w_in[:, :o_f], w_in[:, o_conv:], w_f], axis=1).astype(BF16)
    bf = jnp.tile(b_forget, reps).reshape(1, LANES)

    q, k, v, lf, z, sga, sgc = _inproj(h.reshape(T, D), row2(norm_mix), w_all, bf)

    cum = _cumsum(lf.reshape(B, S, LANES))
    c6 = cum[..., :2 * N_CSPLIT * H].reshape(B, S, 2 * N_CSPLIT, H).transpose(0, 3, 1, 2)
    ones = jnp.ones((B, H, S, N_CSPLIT), BF16)
    zpad = jnp.zeros((B, H, S, HEAD_DIM - 2 * N_CSPLIT), BF16)
    heads = lambda a: a.reshape(B, S, H, HEAD_DIM).transpose(0, 2, 1, 3)
    qa = jnp.concatenate([heads(q), c6[..., :N_CSPLIT], ones, zpad], axis=-1)
    ka = jnp.concatenate([heads(k), ones, c6[..., N_CSPLIT:], zpad], axis=-1)
    attn = _attention(qa, ka, heads(v))

    cw = jnp.concatenate([conv_w, jnp.zeros((CONV_HALO - CONV_K, CONV_WIDTH), F32)], axis=0)
    wr = jnp.concatenate([w_router, jnp.zeros((D, LANES - N_EXPERTS), F32)], axis=1).astype(BF16)
    br = jnp.concatenate([b_router, jnp.full((LANES - N_EXPERTS,), NEG_INF, F32)]).reshape(1, LANES)
    h1, xn2, logits = _mix(
        z.reshape(B, S, CONV_WIDTH), cw, row2(conv_b), row2(conv_ln_g), row2(conv_ln_b), attn,
        sga.reshape(B, S, D), sgc.reshape(B, S, D), h, w_attn_out.astype(BF16), w_conv_out.astype(BF16),
        w_o.astype(BF16), row2(norm_ffn), wr, br)

    rt, counts = _route(logits.reshape(T, LANES))
    cnt = counts[0, :N_EXPERTS].astype(jnp.int32)
    padded = ((cnt + MOE_BLK - 1) // MOE_BLK) * MOE_BLK
    pad_end = jnp.cumsum(padded)
    pad_start = pad_end - padded
    n_blocks = -(-(T * TOP_K) // MOE_BLK) + N_EXPERTS
    n_rows = n_blocks * MOE_BLK
    ps = jnp.concatenate([pad_start.astype(F32), jnp.zeros((LANES - N_EXPERTS,), F32)]).reshape(1, LANES)
    dest = _dest(rt, ps)[:, :TOP_K]
    n_used = pad_end[-1] // MOE_BLK
    blk = jnp.arange(n_blocks, dtype=jnp.int32)
    block_expert = jnp.minimum(
        jnp.sum((pad_end[None, :] <= (blk * MOE_BLK)[:, None]).astype(jnp.int32), axis=1), N_EXPERTS - 1)
    tails = jnp.where(cnt > 0, pad_end - MOE_BLK, n_used * MOE_BLK)
    fill_rows = jnp.concatenate([tails, (n_used + blk) * MOE_BLK]).astype(jnp.int32)
    nfill = (N_EXPERTS + n_blocks - n_used).astype(jnp.int32).reshape(1)

    x_pad = _dispatch(fill_rows, nfill, dest.reshape(T // DISPATCH_TILE, 1, DISPATCH_TILE * TOP_K),
                      xn2.reshape(T, D), n_rows)
    y_pad = _experts(block_expert, n_used.astype(jnp.int32).reshape(1), x_pad,
                     w_gate.astype(BF16), b_gate.reshape(N_EXPERTS, 1, D), w_up.astype(BF16),
                     b_up.reshape(N_EXPERTS, 1, D), w_down.astype(BF16), b_down.reshape(N_EXPERTS, 1, D))
    return _tail, (dest.reshape(T // COMBINE_TILE, 1, COMBINE_TILE * TOP_K), y_pad, rt, h1.reshape(T, D),
                   p_l.reshape(T, PLE_DIM), row2(norm_ple), w_ple_gate.astype(BF16), w_ple_proj.astype(BF16))


def kernel(x, p, norm_mix, w_in, b_forget, w_attn_out, conv_w, conv_b, conv_ln_g, conv_ln_b, w_conv_out, w_o,
           norm_ffn, w_router, b_router, w_gate, b_gate, w_up, b_up, w_down, b_down, norm_ple, w_ple_gate,
           w_ple_proj, norm_final):
    B, S, D = x.shape
    assert p.shape[0] == 1 and D == D_MODEL
    tail, args = _layer(x, p[0], norm_mix[0], w_in[0], b_forget[0], w_attn_out[0], conv_w[0], conv_b[0],
                        conv_ln_g[0], conv_ln_b[0], w_conv_out[0], w_o[0], norm_ffn[0], w_router[0],
                        b_router[0], w_gate[0], b_gate[0], w_up[0], b_up[0], w_down[0], b_down[0],
                        norm_ple[0], w_ple_gate[0], w_ple_proj[0])
    return tail(*args, norm_final.reshape(1, D)).reshape(B, S, D)
```

```python
import functools

import jax
import jax.numpy as jnp
import numpy as np
from jax import lax
from jax.experimental import pallas as pl
from jax.experimental.pallas import tpu as pltpu

F32 = jnp.float32
BF16 = jnp.bfloat16

D_MODEL = 1024
HEAD_DIM = 64
N_HEADS = 8
ATTN_WIDTH = N_HEADS * HEAD_DIM
CONV_WIDTH = 512
CONV_K = 31
N_EXPERTS = 32
TOP_K = 4
SWIGLU_LIMIT = 7.0
SWIGLU_ALPHA = 1.702
PLE_DIM = 256
RMS_EPS = 1e-6
LN_EPS = 1e-5
NEG_INF = -1e30

LANES = 128
VMEM_LIMIT = 56 * 1024 * 1024

ROW_TILE = 512
ATTN_T = 512
HEAD_PAD = 128
CONV_HALO = 32
MOE_BLK = 512
DISPATCH_TILE = 256
COMBINE_TILE = 128


def _cp(sem):
    return pltpu.CompilerParams(dimension_semantics=sem, vmem_limit_bytes=VMEM_LIMIT)


def _dot(a, b):
    return jnp.dot(a, b, preferred_element_type=F32)


def _sigmoid(x):
    return 1.0 / (1.0 + jnp.exp(-x))


def _rms(x, g):
    return x * lax.rsqrt(jnp.mean(x * x, axis=-1, keepdims=True) + RMS_EPS) * g


_C_QKV = 3 * ATTN_WIDTH
_C_CONV = _C_QKV + 2 * CONV_WIDTH
_C_GA = _C_CONV + D_MODEL
_C_GC = _C_GA + D_MODEL
_C_END = _C_GC + LANES


_HP_ALL = N_HEADS * HEAD_PAD
_N_CSPLIT = 3


def _split3(x):
    hi = x.astype(BF16)
    r1 = x - hi.astype(F32)
    mid = r1.astype(BF16)
    lo = (r1 - mid.astype(F32)).astype(BF16)
    return hi, mid, lo


def _store_heads(o_ref, src, aux):
    low = lax.broadcasted_iota(jnp.int32, (src.shape[0], LANES), 1) < HEAD_DIM
    for h in range(N_HEADS):
        chunk = src[:, (h // 2) * LANES:(h // 2 + 1) * LANES]
        if h % 2:
            chunk = pltpu.roll(chunk, HEAD_DIM, 1)
        o_ref[0, :, h * HEAD_PAD:(h + 1) * HEAD_PAD] = jnp.where(
            low, chunk, aux[:, h * HEAD_PAD:(h + 1) * HEAD_PAD]).astype(BF16)


def _inproj_kernel(x_ref, g_ref, w_ref, bf_ref, selq_ref, selk_ref, auxq_ref, auxk_ref, auxv_ref,
                   qa_ref, ka_ref, va_ref, z_ref, sga_ref, sgc_ref, carry_ref):
    tm = x_ref.shape[1]

    @pl.when(pl.program_id(1) == 0)
    def _():
        carry_ref[...] = jnp.zeros_like(carry_ref)

    xn = _rms(x_ref[0], g_ref[...]).astype(BF16)

    f = _dot(xn, w_ref[:, _C_GC:_C_END]) + bf_ref[...]
    lf = jnp.minimum(f, 0.0) - jnp.log(1.0 + jnp.exp(-jnp.abs(f)))
    r = lax.broadcasted_iota(jnp.int32, (tm, tm), 0)
    c = lax.broadcasted_iota(jnp.int32, (tm, tm), 1)
    tri = jnp.where(c <= r, 1.0, 0.0).astype(BF16)
    hi, mid, lo = _split3(lf)
    cs = _dot(tri, hi) + _dot(tri, mid) + _dot(tri, lo) + carry_ref[...]
    carry_ref[...] = cs[tm - 1:tm, :]
    chi, cmid, clo = (a.astype(F32) for a in _split3(cs))
    group = lax.broadcasted_iota(jnp.int32, (tm, LANES), 1) // N_HEADS
    csel = jnp.where(group == 0, chi, jnp.where(group == 1, cmid, jnp.where(
        group == 2, clo, jnp.where(group == 3, -chi, jnp.where(group == 4, -cmid, -clo))))).astype(BF16)
    aux_q = _dot(csel, selq_ref[...]) + auxq_ref[...]
    aux_k = _dot(csel, selk_ref[...]) + auxk_ref[...]

    qkv = _dot(xn, w_ref[:, 0:_C_QKV])
    _store_heads(qa_ref, qkv[:, 0:ATTN_WIDTH], aux_q)
    _store_heads(ka_ref, qkv[:, ATTN_WIDTH:2 * ATTN_WIDTH], aux_k)
    _store_heads(va_ref, qkv[:, 2 * ATTN_WIDTH:3 * ATTN_WIDTH], auxv_ref[...])
    cv = _dot(xn, w_ref[:, _C_QKV:_C_CONV])
    z_ref[0] = cv[:, 0:CONV_WIDTH] * _sigmoid(cv[:, CONV_WIDTH:])
    sga_ref[0] = _sigmoid(_dot(xn, w_ref[:, _C_CONV:_C_GA])).astype(BF16)
    sgc_ref[0] = _sigmoid(_dot(xn, w_ref[:, _C_GA:_C_GC])).astype(BF16)


def _inproj(x, g, w, bf, selq, selk, auxq, auxk, auxv):
    B, S, _ = x.shape
    tm = ROW_TILE
    row = lambda n: pl.BlockSpec((1, tm, n), lambda b, i: (b, i, 0))
    full = lambda a: pl.BlockSpec(a.shape, lambda b, i: (0,) * a.ndim, pipeline_mode=pl.Buffered(1))
    sds = lambda n, dt: jax.ShapeDtypeStruct((B, S, n), dt)
    return pl.pallas_call(
        _inproj_kernel,
        grid=(B, S // tm),
        in_specs=[row(D_MODEL)] + [full(a) for a in (g, w, bf, selq, selk, auxq, auxk, auxv)],
        out_specs=[row(_HP_ALL), row(_HP_ALL), row(_HP_ALL), row(CONV_WIDTH), row(D_MODEL), row(D_MODEL)],
        out_shape=[sds(_HP_ALL, BF16)] * 3 + [sds(CONV_WIDTH, F32), sds(D_MODEL, BF16), sds(D_MODEL, BF16)],
        scratch_shapes=[pltpu.VMEM((1, LANES), F32)],
        compiler_params=_cp(("parallel", "arbitrary")),
        name="inproj",
    )(x, g, w, bf, selq, selk, auxq, auxk, auxv)


def _bias_routing():
    selq = np.zeros((LANES, _HP_ALL), np.float32)
    selk = np.zeros((LANES, _HP_ALL), np.float32)
    auxq = np.zeros((1, _HP_ALL), np.float32)
    auxk = np.zeros((1, _HP_ALL), np.float32)
    auxv = np.zeros((1, _HP_ALL), np.float32)
    for h in range(N_HEADS):
        base = h * HEAD_PAD + HEAD_DIM
        for g in range(_N_CSPLIT):
            selq[g * N_HEADS + h, base + g] = 1.0
            auxk[0, base + g] = 1.0
            auxq[0, base + _N_CSPLIT + g] = 1.0
            selk[(_N_CSPLIT + g) * N_HEADS + h, base + _N_CSPLIT + g] = 1.0
        auxv[0, base] = 1.0
    return (jnp.asarray(selq, BF16), jnp.asarray(selk, BF16), jnp.asarray(auxq), jnp.asarray(auxk),
            jnp.asarray(auxv))


_HEADS_PER_STEP = 2


def _attn_kernel(qa_ref, ka_ref, va_ref, o_ref):
    i = pl.program_id(2)
    t = ATTN_T
    dn = (((1,), (1,)), ((), ()))
    nc = t // LANES

    def tile(j, carry, masked):
        r0 = pl.multiple_of(j * t, t)
        new = []
        for hh in range(_HEADS_PER_STEP):
            m, acc = carry[hh]
            lanes = slice(hh * HEAD_PAD, (hh + 1) * HEAD_PAD)
            s = lax.dot_general(qa_ref[0, :, lanes], ka_ref[0, pl.ds(r0, t), lanes], dn,
                                preferred_element_type=F32)
            if masked:
                row = lax.broadcasted_iota(jnp.int32, (t, t), 0)
                col = lax.broadcasted_iota(jnp.int32, (t, t), 1)
                s = jnp.where(col <= row, s, NEG_INF)
            sc = [s[:, c * LANES:(c + 1) * LANES] for c in range(nc)]
            mloc = functools.reduce(jnp.maximum, sc)
            m_new = jnp.maximum(m, jnp.max(mloc, axis=-1, keepdims=True))
            p = jnp.concatenate([jnp.exp(x - m_new).astype(BF16) for x in sc], axis=1)
            acc = jnp.exp(m - m_new) * acc + _dot(p, va_ref[0, pl.ds(r0, t), lanes])
            new.append((m_new, acc))
        return tuple(new)

    init = tuple((jnp.full((t, LANES), NEG_INF, F32), jnp.zeros((t, HEAD_PAD), F32))
                 for _ in range(_HEADS_PER_STEP))
    carry = lax.fori_loop(0, i, lambda j, c: tile(j, c, False), init)
    carry = tile(i, carry, True)
    outs = [acc[:, 0:HEAD_DIM] / acc[:, HEAD_DIM:HEAD_DIM + 1] for _, acc in carry]
    o_ref[0] = jnp.concatenate(outs, axis=-1).astype(BF16)


def _attention(qa, ka, va):
    B, S, _ = qa.shape
    t = ATTN_T
    w = _HEADS_PER_STEP * HEAD_PAD
    return pl.pallas_call(
        _attn_kernel,
        grid=(B, N_HEADS // _HEADS_PER_STEP, S // t),
        in_specs=[pl.BlockSpec((1, t, w), lambda b, h, i: (b, i, h)),
                  pl.BlockSpec((1, S, w), lambda b, h, i: (b, 0, h)),
                  pl.BlockSpec((1, S, w), lambda b, h, i: (b, 0, h))],
        out_specs=pl.BlockSpec((1, t, _HEADS_PER_STEP * HEAD_DIM), lambda b, h, i: (b, i, h)),
        out_shape=jax.ShapeDtypeStruct((B, S, ATTN_WIDTH), BF16),
        compiler_params=_cp(("parallel", "parallel", "arbitrary")),
        name="fox_attention",
    )(qa, ka, va)


def _mix_kernel(z_ref, zprev_ref, cw_ref, cb_ref, lng_ref, lnb_ref, attn_ref, sga_ref, sgc_ref, x_ref,
                wa_ref, wc_ref, wo_ref, nf_ref, wr_ref, br_ref, h_ref, xn_ref, lg_ref, zs_ref, cv_ref):
    ts = z_ref.shape[1]
    i = pl.program_id(1)
    halo = zprev_ref[0]
    zs_ref[0:CONV_HALO, :] = jnp.where(i > 0, halo, 0.0)
    zs_ref[CONV_HALO:CONV_HALO + ts, :] = z_ref[0]
    base = CONV_HALO - (CONV_K - 1)
    rc = 128
    for c0 in range(0, CONV_WIDTH, LANES):
        for r0 in range(0, ts, rc):
            acc = jnp.zeros((rc, LANES), F32)
            for j in range(CONV_K):
                acc = acc + cw_ref[j:j + 1, c0:c0 + LANES] * zs_ref[r0 + base + j:r0 + base + j + rc, c0:c0 + LANES]
            cv_ref[r0:r0 + rc, c0:c0 + LANES] = acc
    cv = cv_ref[...] + cb_ref[...]
    mu = jnp.mean(cv, axis=-1, keepdims=True)
    d = cv - mu
    var = jnp.mean(d * d, axis=-1, keepdims=True)
    y = d * lax.rsqrt(var + LN_EPS) * lng_ref[...] + lnb_ref[...]
    act = (y * _sigmoid(y)).astype(BF16)
    branch_c = _dot(act, wc_ref[...])
    branch_a = _dot(attn_ref[0], wa_ref[...])
    merged = sga_ref[0].astype(F32) * branch_a + sgc_ref[0].astype(F32) * branch_c
    h = x_ref[0] + _dot(merged.astype(BF16), wo_ref[...])
    h_ref[0] = h
    xn = _rms(h, nf_ref[...])
    xn_ref[0] = xn
    lg_ref[0] = _dot(xn.astype(BF16), wr_ref[...]) + br_ref[...]


def _mix(z, cw, cb, lng, lnb, attn, sga, sgc, x, wa, wc, wo, nf, wr, br):
    B, S, _ = x.shape
    ts = ROW_TILE
    hpb = ts // CONV_HALO
    row = lambda n: pl.BlockSpec((1, ts, n), lambda b, i: (b, i, 0))
    full = lambda a: pl.BlockSpec(a.shape, lambda b, i: (0,) * a.ndim)
    prev = pl.BlockSpec((1, CONV_HALO, CONV_WIDTH), lambda b, i: (b, jnp.maximum(i * hpb - 1, 0), 0))
    return pl.pallas_call(
        _mix_kernel,
        grid=(B, S // ts),
        in_specs=[row(CONV_WIDTH), prev, full(cw), full(cb), full(lng), full(lnb), row(ATTN_WIDTH),
                  row(D_MODEL), row(D_MODEL), row(D_MODEL), full(wa), full(wc), full(wo), full(nf),
                  full(wr), full(br)],
        out_specs=[row(D_MODEL), row(D_MODEL), row(LANES)],
        out_shape=[jax.ShapeDtypeStruct((B, S, D_MODEL), F32), jax.ShapeDtypeStruct((B, S, D_MODEL), F32),
                   jax.ShapeDtypeStruct((B, S, LANES), F32)],
        scratch_shapes=[pltpu.VMEM((CONV_HALO + ts, CONV_WIDTH), F32), pltpu.VMEM((ts, CONV_WIDTH), F32)],
        compiler_params=_cp(("parallel", "parallel")),
        name="mix_merge",
    )(z, z, cw, cb, lng, lnb, attn, sga, sgc, x, wa, wc, wo, nf, wr, br)


def _route_kernel(lg_ref, o_ref, cnt_ref, carry_ref):
    tr = lg_ref.shape[0]

    @pl.when(pl.program_id(0) == 0)
    def _():
        carry_ref[...] = jnp.zeros_like(carry_ref)

    lane = lax.broadcasted_iota(jnp.int32, (tr, LANES), 1).astype(F32)
    vals = lg_ref[...]
    top_v, top_i, hots = [], [], []
    for _ in range(TOP_K):
        m = jnp.max(vals, axis=-1, keepdims=True)
        idx = jnp.min(jnp.where(vals == m, lane, float(LANES)), axis=-1, keepdims=True)
        hot = lane == idx
        vals = jnp.where(hot, -jnp.inf, vals)
        top_v.append(m)
        top_i.append(idx)
        hots.append(hot)
    ex = [jnp.exp(v - top_v[0]) for v in top_v]
    den = ex[0] + ex[1] + ex[2] + ex[3]
    onehot = jnp.zeros((tr, LANES), F32)
    for hot in hots:
        onehot = onehot + jnp.where(hot, 1.0, 0.0)
    r = lax.broadcasted_iota(jnp.int32, (tr, tr), 0)
    c = lax.broadcasted_iota(jnp.int32, (tr, tr), 1)
    strict = jnp.where(c < r, 1.0, 0.0).astype(BF16)
    before = _dot(strict, onehot.astype(BF16)) + carry_ref[...]
    out = jnp.zeros((tr, LANES), F32)
    for k in range(TOP_K):
        rank = jnp.sum(jnp.where(hots[k], before, 0.0), axis=-1, keepdims=True)
        out = jnp.where(lane == k, top_i[k], out)
        out = jnp.where(lane == TOP_K + k, rank, out)
        out = jnp.where(lane == 2 * TOP_K + k, ex[k] / den, out)
    o_ref[...] = out
    total = carry_ref[...] + jnp.sum(onehot, axis=0, keepdims=True)
    carry_ref[...] = total
    cnt_ref[...] = total


def _route(lg):
    T = lg.shape[0]
    tr = ROW_TILE
    return pl.pallas_call(
        _route_kernel,
        grid=(T // tr,),
        in_specs=[pl.BlockSpec((tr, LANES), lambda i: (i, 0))],
        out_specs=[pl.BlockSpec((tr, LANES), lambda i: (i, 0)), pl.BlockSpec((1, LANES), lambda i: (0, 0))],
        out_shape=[jax.ShapeDtypeStruct((T, LANES), F32), jax.ShapeDtypeStruct((1, LANES), F32)],
        scratch_shapes=[pltpu.VMEM((1, LANES), F32)],
        compiler_params=_cp(("arbitrary",)),
        name="route_topk",
    )(lg)


def _dest_kernel(rt_ref, ps_ref, o_ref):
    tr = rt_ref.shape[0]
    lane = lax.broadcasted_iota(jnp.int32, (tr, LANES), 1).astype(F32)
    rt = rt_ref[...]
    out = jnp.zeros((tr, LANES), F32)
    for k in range(TOP_K):
        start = jnp.sum(jnp.where(lane == rt[:, k:k + 1], ps_ref[...], 0.0), axis=-1, keepdims=True)
        out = jnp.where(lane == k, start + rt[:, TOP_K + k:TOP_K + k + 1], out)
    o_ref[...] = out.astype(jnp.int32)


def _dest(rt, pad_start):
    T = rt.shape[0]
    tr = ROW_TILE
    return pl.pallas_call(
        _dest_kernel,
        grid=(T // tr,),
        in_specs=[pl.BlockSpec((tr, LANES), lambda i: (i, 0)), pl.BlockSpec((1, LANES), lambda i: (0, 0))],
        out_specs=pl.BlockSpec((tr, LANES), lambda i: (i, 0)),
        out_shape=jax.ShapeDtypeStruct((T, LANES), jnp.int32),
        compiler_params=_cp(("parallel",)),
        name="route_dest",
    )(rt, pad_start)


def _row_copy(src_ref, s, dst_ref, d, sem):
    return pltpu.make_async_copy(src_ref.at[pl.ds(s, 1)], dst_ref.at[pl.ds(d, 1)], sem)


def _dispatch_kernel(tail_ref, nfill_ref, dest_ref, x_ref, o_ref, zero_ref, sem):
    tb = x_ref.shape[0]
    i = pl.program_id(0)

    @pl.when(i == 0)
    def _():
        zero_ref[...] = jnp.zeros_like(zero_ref)

        def fill(n):
            return pltpu.make_async_copy(zero_ref, o_ref.at[pl.ds(pl.multiple_of(tail_ref[n], MOE_BLK), MOE_BLK)], sem)

        lax.fori_loop(0, nfill_ref[0], lambda n, c: (fill(n).start(), c)[1], 0)
        lax.fori_loop(0, nfill_ref[0], lambda n, c: (fill(n).wait(), c)[1], 0)

    def start(r, c):
        for k in range(TOP_K):
            _row_copy(x_ref, r, o_ref, dest_ref[0, 0, r * TOP_K + k], sem).start()
        return c

    def wait(r, c):
        for k in range(TOP_K):
            _row_copy(x_ref, r, o_ref, dest_ref[0, 0, r * TOP_K + k], sem).wait()
        return c

    lax.fori_loop(0, tb, start, 0)
    lax.fori_loop(0, tb, wait, 0)


def _dispatch(tail_rows, nfill, dest_tiles, xn, n_rows):
    T = xn.shape[0]
    tb = DISPATCH_TILE
    return pl.pallas_call(
        _dispatch_kernel,
        grid_spec=pltpu.PrefetchScalarGridSpec(
            num_scalar_prefetch=2,
            grid=(T // tb,),
            in_specs=[pl.BlockSpec((1, 1, tb * TOP_K), lambda i, *_: (i, 0, 0), memory_space=pltpu.SMEM),
                      pl.BlockSpec((tb, D_MODEL), lambda i, *_: (i, 0))],
            out_specs=pl.BlockSpec(memory_space=pl.ANY),
            scratch_shapes=[pltpu.VMEM((MOE_BLK, D_MODEL), F32), pltpu.SemaphoreType.DMA(())],
        ),
        out_shape=jax.ShapeDtypeStruct((n_rows, D_MODEL), F32),
        compiler_params=_cp(("arbitrary",)),
        name="moe_dispatch",
    )(tail_rows, nfill, dest_tiles, xn)


def _expert_kernel(be_ref, nused_ref, x_ref, wg_ref, bg_ref, wu_ref, bu_ref, wd_ref, bd_ref, y_ref,
                   wgb_ref, wub_ref, wdb_ref):
    i = pl.program_id(0)

    @pl.when((i == 0) | (be_ref[i] != be_ref[jnp.maximum(i - 1, 0)]))
    def _():
        wgb_ref[...] = wg_ref[0].astype(BF16)
        wub_ref[...] = wu_ref[0].astype(BF16)
        wdb_ref[...] = wd_ref[0].astype(BF16)

    @pl.when(i < nused_ref[0])
    def _():
        xb = x_ref[...].astype(BF16)
        g = _dot(xb, wgb_ref[...]) + bg_ref[0]
        u = _dot(xb, wub_ref[...]) + bu_ref[0]
        g = jnp.minimum(g, SWIGLU_LIMIT)
        u = jnp.clip(u, -SWIGLU_LIMIT, SWIGLU_LIMIT)
        glu = g * _sigmoid(SWIGLU_ALPHA * g)
        y_ref[...] = _dot(((u + 1.0) * glu).astype(BF16), wdb_ref[...]) + bd_ref[0]

    @pl.when(i >= nused_ref[0])
    def _():
        y_ref[...] = jnp.zeros_like(y_ref)


def _experts(block_expert, nused, x_pad, wg, bg, wu, bu, wd, bd):
    n_rows = x_pad.shape[0]
    wspec = pl.BlockSpec((1, D_MODEL, D_MODEL), lambda i, be, nu: (be[i], 0, 0))
    bspec = pl.BlockSpec((1, 1, D_MODEL), lambda i, be, nu: (be[i], 0, 0))
    xspec = pl.BlockSpec((MOE_BLK, D_MODEL), lambda i, be, nu: (i, 0))
    return pl.pallas_call(
        _expert_kernel,
        grid_spec=pltpu.PrefetchScalarGridSpec(
            num_scalar_prefetch=2,
            grid=(n_rows // MOE_BLK,),
            in_specs=[xspec, wspec, bspec, wspec, bspec, wspec, bspec],
            out_specs=xspec,
            scratch_shapes=[pltpu.VMEM((D_MODEL, D_MODEL), BF16)] * 3,
        ),
        out_shape=jax.ShapeDtypeStruct((n_rows, D_MODEL), F32),
        compiler_params=_cp(("arbitrary",)),
        name="moe_experts",
    )(block_expert, nused, x_pad, wg, bg, wu, bu, wd, bd)


def _tail_kernel(dest_ref, y_ref, rt_ref, h_ref, p_ref, npl_ref, wpg_ref, wpp_ref, nfin_ref, o_ref, buf_ref, sem):
    tc = h_ref.shape[0]

    def start(r, c):
        for k in range(TOP_K):
            _row_copy(y_ref, dest_ref[0, 0, r * TOP_K + k], buf_ref.at[k], r, sem).start()
        return c

    def wait(r, c):
        for k in range(TOP_K):
            _row_copy(y_ref, dest_ref[0, 0, r * TOP_K + k], buf_ref.at[k], r, sem).wait()
        return c

    lax.fori_loop(0, tc, start, 0)
    lax.fori_loop(0, tc, wait, 0)
    rt = rt_ref[...]
    h = h_ref[...]
    for k in range(TOP_K):
        h = h + buf_ref[k] * rt[:, 2 * TOP_K + k:2 * TOP_K + k + 1]
    pg = _sigmoid(_dot(_rms(h, npl_ref[...]).astype(BF16), wpg_ref[...]))
    h = h + pg * _dot(p_ref[...].astype(BF16), wpp_ref[...])
    o_ref[...] = _rms(h, nfin_ref[...])


def _tail(dest_tiles, y_pad, rt, h1, p2, npl, wpg, wpp, nfin):
    T = h1.shape[0]
    tc = COMBINE_TILE
    row = lambda n: pl.BlockSpec((tc, n), lambda i: (i, 0))
    full = lambda a: pl.BlockSpec(a.shape, lambda i: (0,) * a.ndim)
    return pl.pallas_call(
        _tail_kernel,
        grid=(T // tc,),
        in_specs=[pl.BlockSpec((1, 1, tc * TOP_K), lambda i: (i, 0, 0), memory_space=pltpu.SMEM),
                  pl.BlockSpec(memory_space=pl.ANY), row(LANES), row(D_MODEL), row(PLE_DIM),
                  full(npl), full(wpg), full(wpp), full(nfin)],
        out_specs=row(D_MODEL),
        out_shape=jax.ShapeDtypeStruct((T, D_MODEL), F32),
        scratch_shapes=[pltpu.VMEM((TOP_K, tc, D_MODEL), F32), pltpu.SemaphoreType.DMA(())],
        compiler_params=_cp(("arbitrary",)),
        name="moe_combine_tail",
    )(dest_tiles, y_pad, rt, h1, p2, npl, wpg, wpp, nfin)


def _layer(h, p_l, norm_mix, w_in, b_forget, w_attn_out, conv_w, conv_b, conv_ln_g, conv_ln_b, w_conv_out, w_o,
           norm_ffn, w_router, b_router, w_gate, b_gate, w_up, b_up, w_down, b_down, norm_ple, w_ple_gate,
           w_ple_proj):
    B, S, D = h.shape
    T = B * S
    row2 = lambda a: a.reshape(1, -1)

    o_f = 3 * ATTN_WIDTH
    o_conv = o_f + N_HEADS
    reps = LANES // N_HEADS
    w_f = jnp.tile(w_in[:, o_f:o_conv], (1, reps))
    w_all = jnp.concatenate([w_in[:, :ATTN_WIDTH] * (HEAD_DIM ** -0.5), w_in[:, ATTN_WIDTH:o_f],
                             w_in[:, o_conv:], w_f], axis=1).astype(BF16)
    bf = jnp.tile(b_forget, reps).reshape(1, LANES)

    qa, ka, va, z, sga, sgc = _inproj(h, row2(norm_mix), w_all, bf, *_bias_routing())
    attn = _attention(qa, ka, va)

    cw = jnp.concatenate([conv_w, jnp.zeros((CONV_HALO - CONV_K, CONV_WIDTH), F32)], axis=0)
    wr = jnp.concatenate([w_router, jnp.zeros((D, LANES - N_EXPERTS), F32)], axis=1).astype(BF16)
    br = jnp.concatenate([b_router, jnp.full((LANES - N_EXPERTS,), NEG_INF, F32)]).reshape(1, LANES)
    h1, xn2, logits = _mix(
        z, cw, row2(conv_b), row2(conv_ln_g), row2(conv_ln_b), attn,
        sga, sgc, h, w_attn_out.astype(BF16), w_conv_out.astype(BF16),
        w_o.astype(BF16), row2(norm_ffn), wr, br)

    rt, counts = _route(logits.reshape(T, LANES))
    cnt = counts[0, :N_EXPERTS].astype(jnp.int32)
    padded = ((cnt + MOE_BLK - 1) // MOE_BLK) * MOE_BLK
    pad_end = jnp.cumsum(padded)
    pad_start = pad_end - padded
    n_blocks = -(-(T * TOP_K) // MOE_BLK) + N_EXPERTS
    n_rows = n_blocks * MOE_BLK
    ps = jnp.concatenate([pad_start.astype(F32), jnp.zeros((LANES - N_EXPERTS,), F32)]).reshape(1, LANES)
    dest = _dest(rt, ps)[:, :TOP_K]
    n_used = pad_end[-1] // MOE_BLK
    blk = jnp.arange(n_blocks, dtype=jnp.int32)
    block_expert = jnp.minimum(
        jnp.sum((pad_end[None, :] <= (blk * MOE_BLK)[:, None]).astype(jnp.int32), axis=1), N_EXPERTS - 1)
    tails = jnp.where(cnt > 0, pad_end - MOE_BLK, n_used * MOE_BLK)
    fill_rows = jnp.concatenate([tails, (n_used + blk) * MOE_BLK]).astype(jnp.int32)
    nfill = (N_EXPERTS + n_blocks - n_used).astype(jnp.int32).reshape(1)

    x_pad = _dispatch(fill_rows, nfill, dest.reshape(T // DISPATCH_TILE, 1, DISPATCH_TILE * TOP_K),
                      xn2.reshape(T, D), n_rows)
    y_pad = _experts(block_expert, n_used.astype(jnp.int32).reshape(1), x_pad,
                     w_gate, b_gate.reshape(N_EXPERTS, 1, D), w_up,
                     b_up.reshape(N_EXPERTS, 1, D), w_down, b_down.reshape(N_EXPERTS, 1, D))
    return _tail, (dest.reshape(T // COMBINE_TILE, 1, COMBINE_TILE * TOP_K), y_pad, rt, h1.reshape(T, D),
                   p_l.reshape(T, PLE_DIM), row2(norm_ple), w_ple_gate.astype(BF16), w_ple_proj.astype(BF16))


def kernel(x, p, norm_mix, w_in, b_forget, w_attn_out, conv_w, conv_b, conv_ln_g, conv_ln_b, w_conv_out, w_o,
           norm_ffn, w_router, b_router, w_gate, b_gate, w_up, b_up, w_down, b_down, norm_ple, w_ple_gate,
           w_ple_proj, norm_final):
    B, S, D = x.shape
    assert p.shape[0] == 1 and D == D_MODEL
    tail, args = _layer(x, p[0], norm_mix[0], w_in[0], b_forget[0], w_attn_out[0], conv_w[0], conv_b[0],
                        conv_ln_g[0], conv_ln_b[0], w_conv_out[0], w_o[0], norm_ffn[0], w_router[0],
                        b_router[0], w_gate[0], b_gate[0], w_up[0], b_up[0], w_down[0], b_down[0],
                        norm_ple[0], w_ple_gate[0], w_ple_proj[0])
    return tail(*args, norm_final.reshape(1, D)).reshape(B, S, D)
```

```python
import functools

import jax
import jax.numpy as jnp
import numpy as np
from jax import lax
from jax.experimental import pallas as pl
from jax.experimental.pallas import tpu as pltpu

F32 = jnp.float32
BF16 = jnp.bfloat16

D_MODEL = 1024
HEAD_DIM = 64
N_HEADS = 8
ATTN_WIDTH = N_HEADS * HEAD_DIM
CONV_WIDTH = 512
CONV_K = 31
N_EXPERTS = 32
TOP_K = 4
SWIGLU_LIMIT = 7.0
SWIGLU_ALPHA = 1.702
PLE_DIM = 256
RMS_EPS = 1e-6
LN_EPS = 1e-5
NEG_INF = -1e30

LANES = 128
SUBLANES = 8
VMEM_LIMIT = 56 * 1024 * 1024

ROW_TILE = 512
ATTN_T = 512
HEAD_PAD = 128
CONV_HALO = 32
MOE_BLK = 512
MOE_TILE = 256
RUN_ALIGN = 8


def _cp(sem):
    return pltpu.CompilerParams(dimension_semantics=sem, vmem_limit_bytes=VMEM_LIMIT)


def _dot(a, b):
    return jnp.dot(a, b, preferred_element_type=F32)


def _sigmoid(x):
    return 1.0 / (1.0 + jnp.exp(-x))


def _rms(x, g):
    return x * lax.rsqrt(jnp.mean(x * x, axis=-1, keepdims=True) + RMS_EPS) * g


_C_QKV = 3 * ATTN_WIDTH
_C_CONV = _C_QKV + 2 * CONV_WIDTH
_C_GA = _C_CONV + D_MODEL
_C_GC = _C_GA + D_MODEL
_C_END = _C_GC + LANES


_HP_ALL = N_HEADS * HEAD_PAD
_N_CSPLIT = 3


def _split3(x):
    hi = x.astype(BF16)
    r1 = x - hi.astype(F32)
    mid = r1.astype(BF16)
    lo = (r1 - mid.astype(F32)).astype(BF16)
    return hi, mid, lo


def _store_heads(o_ref, src, aux):
    low = lax.broadcasted_iota(jnp.int32, (src.shape[0], LANES), 1) < HEAD_DIM
    for h in range(N_HEADS):
        chunk = src[:, (h // 2) * LANES:(h // 2 + 1) * LANES]
        if h % 2:
            chunk = pltpu.roll(chunk, HEAD_DIM, 1)
        o_ref[0, :, h * HEAD_PAD:(h + 1) * HEAD_PAD] = jnp.where(
            low, chunk, aux[:, h * HEAD_PAD:(h + 1) * HEAD_PAD]).astype(BF16)


def _inproj_kernel(x_ref, g_ref, w_ref, bf_ref, selq_ref, selk_ref, auxq_ref, auxk_ref, auxv_ref,
                   qa_ref, ka_ref, va_ref, z_ref, sga_ref, sgc_ref, carry_ref):
    tm = x_ref.shape[1]

    @pl.when(pl.program_id(1) == 0)
    def _():
        carry_ref[...] = jnp.zeros_like(carry_ref)

    xn = _rms(x_ref[0], g_ref[...]).astype(BF16)

    f = _dot(xn, w_ref[:, _C_GC:_C_END]) + bf_ref[...]
    lf = jnp.minimum(f, 0.0) - jnp.log(1.0 + jnp.exp(-jnp.abs(f)))
    r = lax.broadcasted_iota(jnp.int32, (tm, tm), 0)
    c = lax.broadcasted_iota(jnp.int32, (tm, tm), 1)
    tri = jnp.where(c <= r, 1.0, 0.0).astype(BF16)
    hi, mid, lo = _split3(lf)
    cs = _dot(tri, hi) + _dot(tri, mid) + _dot(tri, lo) + carry_ref[...]
    carry_ref[...] = cs[tm - 1:tm, :]
    chi, cmid, clo = (a.astype(F32) for a in _split3(cs))
    group = lax.broadcasted_iota(jnp.int32, (tm, LANES), 1) // N_HEADS
    csel = jnp.where(group == 0, chi, jnp.where(group == 1, cmid, jnp.where(
        group == 2, clo, jnp.where(group == 3, -chi, jnp.where(group == 4, -cmid, -clo))))).astype(BF16)
    aux_q = _dot(csel, selq_ref[...]) + auxq_ref[...]
    aux_k = _dot(csel, selk_ref[...]) + auxk_ref[...]

    qkv = _dot(xn, w_ref[:, 0:_C_QKV])
    _store_heads(qa_ref, qkv[:, 0:ATTN_WIDTH], aux_q)
    _store_heads(ka_ref, qkv[:, ATTN_WIDTH:2 * ATTN_WIDTH], aux_k)
    _store_heads(va_ref, qkv[:, 2 * ATTN_WIDTH:3 * ATTN_WIDTH], auxv_ref[...])
    cv = _dot(xn, w_ref[:, _C_QKV:_C_CONV])
    z_ref[0] = cv[:, 0:CONV_WIDTH] * _sigmoid(cv[:, CONV_WIDTH:])
    sga_ref[0] = _sigmoid(_dot(xn, w_ref[:, _C_CONV:_C_GA])).astype(BF16)
    sgc_ref[0] = _sigmoid(_dot(xn, w_ref[:, _C_GA:_C_GC])).astype(BF16)


def _inproj(x, g, w, bf, selq, selk, auxq, auxk, auxv):
    B, S, _ = x.shape
    tm = ROW_TILE
    row = lambda n: pl.BlockSpec((1, tm, n), lambda b, i: (b, i, 0))
    full = lambda a: pl.BlockSpec(a.shape, lambda b, i: (0,) * a.ndim, pipeline_mode=pl.Buffered(1))
    sds = lambda n, dt: jax.ShapeDtypeStruct((B, S, n), dt)
    return pl.pallas_call(
        _inproj_kernel,
        grid=(B, S // tm),
        in_specs=[row(D_MODEL)] + [full(a) for a in (g, w, bf, selq, selk, auxq, auxk, auxv)],
        out_specs=[row(_HP_ALL), row(_HP_ALL), row(_HP_ALL), row(CONV_WIDTH), row(D_MODEL), row(D_MODEL)],
        out_shape=[sds(_HP_ALL, BF16)] * 3 + [sds(CONV_WIDTH, F32), sds(D_MODEL, BF16), sds(D_MODEL, BF16)],
        scratch_shapes=[pltpu.VMEM((1, LANES), F32)],
        compiler_params=_cp(("parallel", "arbitrary")),
        name="inproj",
    )(x, g, w, bf, selq, selk, auxq, auxk, auxv)


def _bias_routing():
    selq = np.zeros((LANES, _HP_ALL), np.float32)
    selk = np.zeros((LANES, _HP_ALL), np.float32)
    auxq = np.zeros((1, _HP_ALL), np.float32)
    auxk = np.zeros((1, _HP_ALL), np.float32)
    auxv = np.zeros((1, _HP_ALL), np.float32)
    for h in range(N_HEADS):
        base = h * HEAD_PAD + HEAD_DIM
        for g in range(_N_CSPLIT):
            selq[g * N_HEADS + h, base + g] = 1.0
            auxk[0, base + g] = 1.0
            auxq[0, base + _N_CSPLIT + g] = 1.0
            selk[(_N_CSPLIT + g) * N_HEADS + h, base + _N_CSPLIT + g] = 1.0
        auxv[0, base] = 1.0
    return (jnp.asarray(selq, BF16), jnp.asarray(selk, BF16), jnp.asarray(auxq), jnp.asarray(auxk),
            jnp.asarray(auxv))


_HEADS_PER_STEP = 2


def _attn_kernel(qa_ref, ka_ref, va_ref, o_ref):
    i = pl.program_id(2)
    t = ATTN_T
    dn = (((1,), (1,)), ((), ()))
    nc = t // LANES

    def tile(j, carry, masked):
        r0 = pl.multiple_of(j * t, t)
        new = []
        for hh in range(_HEADS_PER_STEP):
            m, acc = carry[hh]
            lanes = slice(hh * HEAD_PAD, (hh + 1) * HEAD_PAD)
            s = lax.dot_general(qa_ref[0, :, lanes], ka_ref[0, pl.ds(r0, t), lanes], dn,
                                preferred_element_type=F32)
            if masked:
                row = lax.broadcasted_iota(jnp.int32, (t, t), 0)
                col = lax.broadcasted_iota(jnp.int32, (t, t), 1)
                s = jnp.where(col <= row, s, NEG_INF)
            sc = [s[:, c * LANES:(c + 1) * LANES] for c in range(nc)]
            mloc = functools.reduce(jnp.maximum, sc)
            m_new = jnp.maximum(m, jnp.max(mloc, axis=-1, keepdims=True))
            p = jnp.concatenate([jnp.exp(x - m_new).astype(BF16) for x in sc], axis=1)
            acc = jnp.exp(m - m_new) * acc + _dot(p, va_ref[0, pl.ds(r0, t), lanes])
            new.append((m_new, acc))
        return tuple(new)

    init = tuple((jnp.full((t, LANES), NEG_INF, F32), jnp.zeros((t, HEAD_PAD), F32))
                 for _ in range(_HEADS_PER_STEP))
    carry = lax.fori_loop(0, i, lambda j, c: tile(j, c, False), init)
    carry = tile(i, carry, True)
    outs = [acc[:, 0:HEAD_DIM] / acc[:, HEAD_DIM:HEAD_DIM + 1] for _, acc in carry]
    o_ref[0] = jnp.concatenate(outs, axis=-1).astype(BF16)


def _attention(qa, ka, va):
    B, S, _ = qa.shape
    t = ATTN_T
    w = _HEADS_PER_STEP * HEAD_PAD
    return pl.pallas_call(
        _attn_kernel,
        grid=(B, N_HEADS // _HEADS_PER_STEP, S // t),
        in_specs=[pl.BlockSpec((1, t, w), lambda b, h, i: (b, i, h)),
                  pl.BlockSpec((1, S, w), lambda b, h, i: (b, 0, h)),
                  pl.BlockSpec((1, S, w), lambda b, h, i: (b, 0, h))],
        out_specs=pl.BlockSpec((1, t, _HEADS_PER_STEP * HEAD_DIM), lambda b, h, i: (b, i, h)),
        out_shape=jax.ShapeDtypeStruct((B, S, ATTN_WIDTH), BF16),
        compiler_params=_cp(("parallel", "parallel", "arbitrary")),
        name="fox_attention",
    )(qa, ka, va)


def _mix_kernel(z_ref, zprev_ref, cw_ref, cb_ref, lng_ref, lnb_ref, attn_ref, sga_ref, sgc_ref, x_ref,
                wa_ref, wc_ref, wo_ref, nf_ref, wr_ref, br_ref, h_ref, xn_ref, lg_ref, zs_ref, cv_ref, ph_ref):
    ts = z_ref.shape[1]
    i = pl.program_id(1)
    halo = zprev_ref[0]
    zs_ref[0:CONV_HALO, :] = jnp.where(i > 0, halo, 0.0)
    zs_ref[CONV_HALO:CONV_HALO + ts, :] = z_ref[0]
    base = CONV_HALO - (CONV_K - 1)
    for ph in range(SUBLANES):
        rows = ts + SUBLANES * ((CONV_K - 1 - ph) // SUBLANES)
        ph_ref[ph, 0:rows, :] = zs_ref[base + ph:base + ph + rows, :]
    rc = 128
    for c0 in range(0, CONV_WIDTH, LANES):
        for r0 in range(0, ts, rc):
            acc = jnp.zeros((rc, LANES), F32)
            for j in range(CONV_K):
                a0 = r0 + SUBLANES * (j // SUBLANES)
                acc = acc + cw_ref[j:j + 1, c0:c0 + LANES] * ph_ref[j % SUBLANES, a0:a0 + rc, c0:c0 + LANES]
            cv_ref[r0:r0 + rc, c0:c0 + LANES] = acc
    cv = cv_ref[...] + cb_ref[...]
    mu = jnp.mean(cv, axis=-1, keepdims=True)
    d = cv - mu
    var = jnp.mean(d * d, axis=-1, keepdims=True)
    y = d * lax.rsqrt(var + LN_EPS) * lng_ref[...] + lnb_ref[...]
    act = (y * _sigmoid(y)).astype(BF16)
    branch_c = _dot(act, wc_ref[...])
    branch_a = _dot(attn_ref[0], wa_ref[...])
    merged = sga_ref[0].astype(F32) * branch_a + sgc_ref[0].astype(F32) * branch_c
    h = x_ref[0] + _dot(merged.astype(BF16), wo_ref[...])
    h_ref[0] = h
    xn = _rms(h, nf_ref[...])
    xn_ref[0] = xn
    lg_ref[0] = _dot(xn.astype(BF16), wr_ref[...]) + br_ref[...]


def _mix(z, cw, cb, lng, lnb, attn, sga, sgc, x, wa, wc, wo, nf, wr, br):
    B, S, _ = x.shape
    ts = ROW_TILE
    hpb = ts // CONV_HALO
    row = lambda n: pl.BlockSpec((1, ts, n), lambda b, i: (b, i, 0))
    full = lambda a: pl.BlockSpec(a.shape, lambda b, i: (0,) * a.ndim)
    prev = pl.BlockSpec((1, CONV_HALO, CONV_WIDTH), lambda b, i: (b, jnp.maximum(i * hpb - 1, 0), 0))
    return pl.pallas_call(
        _mix_kernel,
        grid=(B, S // ts),
        in_specs=[row(CONV_WIDTH), prev, full(cw), full(cb), full(lng), full(lnb), row(ATTN_WIDTH),
                  row(D_MODEL), row(D_MODEL), row(D_MODEL), full(wa), full(wc), full(wo), full(nf),
                  full(wr), full(br)],
        out_specs=[row(D_MODEL), row(D_MODEL), row(LANES)],
        out_shape=[jax.ShapeDtypeStruct((B, S, D_MODEL), F32), jax.ShapeDtypeStruct((B, S, D_MODEL), F32),
                   jax.ShapeDtypeStruct((B, S, LANES), F32)],
        scratch_shapes=[pltpu.VMEM((CONV_HALO + ts, CONV_WIDTH), F32), pltpu.VMEM((ts, CONV_WIDTH), F32),
                        pltpu.VMEM((SUBLANES, ts + CONV_HALO - SUBLANES, CONV_WIDTH), F32)],
        compiler_params=_cp(("parallel", "parallel")),
        name="mix_merge",
    )(z, z, cw, cb, lng, lnb, attn, sga, sgc, x, wa, wc, wo, nf, wr, br)


def _route_kernel(lg_ref, o_ref, tcnt_ref, tbef_ref, cnt_ref, carry_ref):
    tr = lg_ref.shape[0]

    @pl.when(pl.program_id(0) == 0)
    def _():
        carry_ref[...] = jnp.zeros_like(carry_ref)

    lane = lax.broadcasted_iota(jnp.int32, (tr, LANES), 1).astype(F32)
    vals = lg_ref[...]
    top_v, top_i, hots = [], [], []
    for _ in range(TOP_K):
        m = jnp.max(vals, axis=-1, keepdims=True)
        idx = jnp.min(jnp.where(vals == m, lane, float(LANES)), axis=-1, keepdims=True)
        hot = lane == idx
        vals = jnp.where(hot, -jnp.inf, vals)
        top_v.append(m)
        top_i.append(idx)
        hots.append(hot)
    ex = [jnp.exp(v - top_v[0]) for v in top_v]
    den = ex[0] + ex[1] + ex[2] + ex[3]
    onehot = jnp.zeros((tr, LANES), F32)
    for hot in hots:
        onehot = onehot + jnp.where(hot, 1.0, 0.0)
    r = lax.broadcasted_iota(jnp.int32, (tr, tr), 0)
    c = lax.broadcasted_iota(jnp.int32, (tr, tr), 1)
    strict = jnp.where(c < r, 1.0, 0.0).astype(BF16)
    earlier = _dot(strict, onehot.astype(BF16))
    n_tile = jnp.sum(onehot, axis=0, keepdims=True)
    n_tile = jnp.floor((n_tile + (RUN_ALIGN - 1)) * (1.0 / RUN_ALIGN)) * RUN_ALIGN
    li = lax.broadcasted_iota(jnp.int32, (LANES, LANES), 0)
    lj = lax.broadcasted_iota(jnp.int32, (LANES, LANES), 1)
    lower = jnp.where(li < lj, 1.0, 0.0).astype(BF16)
    run_off = _dot(jnp.broadcast_to(n_tile, (8, LANES)).astype(BF16), lower)[0:1, :]
    within = earlier + run_off
    out = jnp.zeros((tr, LANES), F32)
    for k in range(TOP_K):
        pos = jnp.sum(jnp.where(hots[k], within, 0.0), axis=-1, keepdims=True)
        out = jnp.where(lane == k, top_i[k], out)
        out = jnp.where(lane == TOP_K + k, pos, out)
        out = jnp.where(lane == 2 * TOP_K + k, ex[k] / den, out)
    o_ref[...] = out
    tcnt_ref[0] = n_tile
    tbef_ref[0] = carry_ref[...]
    total = carry_ref[...] + n_tile
    carry_ref[...] = total
    cnt_ref[...] = total


def _route(lg):
    T = lg.shape[0]
    tr = MOE_TILE
    per_tile = pl.BlockSpec((1, 1, LANES), lambda i: (i, 0, 0))
    return pl.pallas_call(
        _route_kernel,
        grid=(T // tr,),
        in_specs=[pl.BlockSpec((tr, LANES), lambda i: (i, 0))],
        out_specs=[pl.BlockSpec((tr, LANES), lambda i: (i, 0)), per_tile, per_tile,
                   pl.BlockSpec((1, LANES), lambda i: (0, 0))],
        out_shape=[jax.ShapeDtypeStruct((T, LANES), F32), jax.ShapeDtypeStruct((T // tr, 1, LANES), F32),
                   jax.ShapeDtypeStruct((T // tr, 1, LANES), F32), jax.ShapeDtypeStruct((1, LANES), F32)],
        scratch_shapes=[pltpu.VMEM((1, LANES), F32)],
        compiler_params=_cp(("arbitrary",)),
        name="route_topk",
    )(lg)


_ALIGN_BITS = RUN_ALIGN.bit_length() - 1
_TILE_ROWS = MOE_TILE * TOP_K + N_EXPERTS * RUN_ALIGN
_CHUNK_BITS = tuple(range(_ALIGN_BITS, MOE_TILE.bit_length()))
_N_CHUNK = len(_CHUNK_BITS)


def _run_copies(tile, slot, run_dst_ref, run_len_ref, run_off_ref, tile_buf, hbm_ref, sems, to_hbm):
    def one_expert(e, c):
        idx = tile * N_EXPERTS + e
        n, dst, off = run_len_ref[idx], run_dst_ref[idx], run_off_ref[idx]
        for s, b in enumerate(_CHUNK_BITS):
            size = 1 << b

            @pl.when((n & size) != 0)
            def _():
                done = (n >> (b + 1)) << (b + 1)
                loc = tile_buf.at[slot, pl.ds(pl.multiple_of(off + done, RUN_ALIGN), size)]
                far = hbm_ref.at[pl.ds(pl.multiple_of(dst + done, RUN_ALIGN), size)]
                src, dstn = (loc, far) if to_hbm else (far, loc)
                pltpu.make_async_copy(src, dstn, sems.at[slot, s]).start()
        return c

    lax.fori_loop(0, N_EXPERTS, one_expert, 0)


def _run_waits(tile, slot, nchunk_ref, tile_buf, hbm_ref, sems, to_hbm):
    for s, b in enumerate(_CHUNK_BITS):
        size = 1 << b
        loc = tile_buf.at[slot, pl.ds(0, size)]
        far = hbm_ref.at[pl.ds(0, size)]
        src, dstn = (loc, far) if to_hbm else (far, loc)
        cp = pltpu.make_async_copy(src, dstn, sems.at[slot, s])
        lax.fori_loop(0, nchunk_ref[tile * _N_CHUNK + s], lambda n, c: (cp.wait(), c)[1], 0)


def _dispatch_kernel(tail_ref, nused_ref, run_dst_ref, run_len_ref, run_off_ref, nchunk_ref,
                     rt_ref, x_ref, o_ref, xs_ref, zero_ref, sems, fill_sem, trail_sem):
    i = pl.program_id(0)
    slot = i % 2
    n_blocks = o_ref.shape[0] // MOE_BLK

    def fill(row, sem):
        return pltpu.make_async_copy(zero_ref, o_ref.at[pl.ds(pl.multiple_of(row, MOE_BLK), MOE_BLK)], sem)

    def trailing(op):
        lax.fori_loop(nused_ref[0], n_blocks, lambda n, c: (op(fill(n * MOE_BLK, trail_sem)), c)[1], 0)

    @pl.when(i == 0)
    def _():
        zero_ref[...] = jnp.zeros_like(zero_ref)
        lax.fori_loop(0, N_EXPERTS, lambda n, c: (fill(tail_ref[n], fill_sem).start(), c)[1], 0)
        trailing(lambda cp: cp.start())
        lax.fori_loop(0, N_EXPERTS, lambda n, c: (fill(tail_ref[n], fill_sem).wait(), c)[1], 0)

    pos_t = rt_ref[...].T
    rows = lax.broadcasted_iota(jnp.int32, (_TILE_ROWS, MOE_TILE), 0).astype(F32)
    place = jnp.zeros((_TILE_ROWS, MOE_TILE), F32)
    for k in range(TOP_K):
        place = place + jnp.where(rows == pos_t[TOP_K + k:TOP_K + k + 1, :], 1.0, 0.0)
    xs_ref[slot] = _dot(place.astype(BF16), x_ref[...].astype(BF16))

    _run_copies(i, slot, run_dst_ref, run_len_ref, run_off_ref, xs_ref, o_ref, sems, True)

    @pl.when(i > 0)
    def _():
        _run_waits(i - 1, 1 - slot, nchunk_ref, xs_ref, o_ref, sems, True)

    @pl.when(i == pl.num_programs(0) - 1)
    def _():
        _run_waits(i, slot, nchunk_ref, xs_ref, o_ref, sems, True)
        trailing(lambda cp: cp.wait())


def _dispatch(tail_rows, nused, run_dst, run_len, run_off, nchunk, rt, xn, n_rows):
    T = xn.shape[0]
    tb = MOE_TILE
    return pl.pallas_call(
        _dispatch_kernel,
        grid_spec=pltpu.PrefetchScalarGridSpec(
            num_scalar_prefetch=6,
            grid=(T // tb,),
            in_specs=[pl.BlockSpec((tb, LANES), lambda i, *_: (i, 0)),
                      pl.BlockSpec((tb, D_MODEL), lambda i, *_: (i, 0))],
            out_specs=pl.BlockSpec(memory_space=pl.ANY),
            scratch_shapes=[pltpu.VMEM((2, _TILE_ROWS, D_MODEL), F32), pltpu.VMEM((MOE_BLK, D_MODEL), F32),
                            pltpu.SemaphoreType.DMA((2, _N_CHUNK)), pltpu.SemaphoreType.DMA(()),
                            pltpu.SemaphoreType.DMA(())],
        ),
        out_shape=jax.ShapeDtypeStruct((n_rows, D_MODEL), F32),
        compiler_params=_cp(("arbitrary",)),
        name="moe_dispatch",
    )(tail_rows, nused, run_dst, run_len, run_off, nchunk, rt, xn)


def _expert_kernel(be_ref, nused_ref, x_ref, wg_ref, bg_ref, wu_ref, bu_ref, wd_ref, bd_ref, y_ref,
                   wgb_ref, wub_ref, wdb_ref):
    i = pl.program_id(0)

    @pl.when((i == 0) | (be_ref[i] != be_ref[jnp.maximum(i - 1, 0)]))
    def _():
        wgb_ref[...] = wg_ref[0].astype(BF16)
        wub_ref[...] = wu_ref[0].astype(BF16)
        wdb_ref[...] = wd_ref[0].astype(BF16)

    @pl.when(i < nused_ref[0])
    def _():
        xb = x_ref[...].astype(BF16)
        g = _dot(xb, wgb_ref[...]) + bg_ref[0]
        u = _dot(xb, wub_ref[...]) + bu_ref[0]
        g = jnp.minimum(g, SWIGLU_LIMIT)
        u = jnp.clip(u, -SWIGLU_LIMIT, SWIGLU_LIMIT)
        glu = g * _sigmoid(SWIGLU_ALPHA * g)
        y_ref[...] = _dot(((u + 1.0) * glu).astype(BF16), wdb_ref[...]) + bd_ref[0]

    @pl.when(i >= nused_ref[0])
    def _():
        y_ref[...] = jnp.zeros_like(y_ref)


def _experts(block_expert, nused, x_pad, wg, bg, wu, bu, wd, bd):
    n_rows = x_pad.shape[0]
    wspec = pl.BlockSpec((1, D_MODEL, D_MODEL), lambda i, be, nu: (be[i], 0, 0))
    bspec = pl.BlockSpec((1, 1, D_MODEL), lambda i, be, nu: (be[i], 0, 0))
    yspec = pl.BlockSpec((MOE_BLK, D_MODEL), lambda i, be, nu: (i, 0))
    xspec = pl.BlockSpec((MOE_BLK, D_MODEL), lambda i, be, nu: (jnp.minimum(i, nu[0] - 1), 0))
    return pl.pallas_call(
        _expert_kernel,
        grid_spec=pltpu.PrefetchScalarGridSpec(
            num_scalar_prefetch=2,
            grid=(n_rows // MOE_BLK,),
            in_specs=[xspec, wspec, bspec, wspec, bspec, wspec, bspec],
            out_specs=yspec,
            scratch_shapes=[pltpu.VMEM((D_MODEL, D_MODEL), BF16)] * 3,
        ),
        out_shape=jax.ShapeDtypeStruct((n_rows, D_MODEL), F32),
        compiler_params=_cp(("arbitrary",)),
        name="moe_experts",
    )(block_expert, nused, x_pad, wg, bg, wu, bu, wd, bd)


def _tail_kernel(run_dst_ref, run_len_ref, run_off_ref, nchunk_ref, y_ref, rt_ref, h_ref, p_ref, npl_ref,
                 wpg_ref, wpp_ref, nfin_ref, o_ref, ys_ref, sems):
    i = pl.program_id(0)
    slot = i % 2
    fetch = functools.partial(_run_copies, run_dst_ref=run_dst_ref, run_len_ref=run_len_ref,
                              run_off_ref=run_off_ref, tile_buf=ys_ref, hbm_ref=y_ref, sems=sems, to_hbm=False)

    @pl.when(i == 0)
    def _():
        ys_ref[...] = jnp.zeros_like(ys_ref)
        fetch(i, slot)

    @pl.when(i + 1 < pl.num_programs(0))
    def _():
        fetch(i + 1, 1 - slot)

    _run_waits(i, slot, nchunk_ref, ys_ref, y_ref, sems, False)

    rt = rt_ref[...]
    cols = lax.broadcasted_iota(jnp.int32, (MOE_TILE, _TILE_ROWS), 1).astype(F32)
    weight = jnp.zeros((MOE_TILE, _TILE_ROWS), F32)
    for k in range(TOP_K):
        weight = weight + jnp.where(cols == rt[:, TOP_K + k:TOP_K + k + 1],
                                    rt[:, 2 * TOP_K + k:2 * TOP_K + k + 1], 0.0)
    h = h_ref[...] + _dot(weight.astype(BF16), ys_ref[slot].astype(BF16))
    pg = _sigmoid(_dot(_rms(h, npl_ref[...]).astype(BF16), wpg_ref[...]))
    h = h + pg * _dot(p_ref[...].astype(BF16), wpp_ref[...])
    o_ref[...] = _rms(h, nfin_ref[...])


def _tail(run_dst, run_len, run_off, nchunk, y_pad, rt, h1, p2, npl, wpg, wpp, nfin):
    T = h1.shape[0]
    tc = MOE_TILE
    row = lambda n: pl.BlockSpec((tc, n), lambda i, *_: (i, 0))
    full = lambda a: pl.BlockSpec(a.shape, lambda i, *_: (0,) * a.ndim)
    return pl.pallas_call(
        _tail_kernel,
        grid_spec=pltpu.PrefetchScalarGridSpec(
            num_scalar_prefetch=4,
            grid=(T // tc,),
            in_specs=[pl.BlockSpec(memory_space=pl.ANY), row(LANES), row(D_MODEL), row(PLE_DIM),
                      full(npl), full(wpg), full(wpp), full(nfin)],
            out_specs=row(D_MODEL),
            scratch_shapes=[pltpu.VMEM((2, _TILE_ROWS, D_MODEL), F32), pltpu.SemaphoreType.DMA((2, _N_CHUNK))],
        ),
        out_shape=jax.ShapeDtypeStruct((T, D_MODEL), F32),
        compiler_params=_cp(("arbitrary",)),
        name="moe_combine_tail",
    )(run_dst, run_len, run_off, nchunk, y_pad, rt, h1, p2, npl, wpg, wpp, nfin)


def _layer(h, p_l, norm_mix, w_in, b_forget, w_attn_out, conv_w, conv_b, conv_ln_g, conv_ln_b, w_conv_out, w_o,
           norm_ffn, w_router, b_router, w_gate, b_gate, w_up, b_up, w_down, b_down, norm_ple, w_ple_gate,
           w_ple_proj):
    B, S, D = h.shape
    T = B * S
    row2 = lambda a: a.reshape(1, -1)

    o_f = 3 * ATTN_WIDTH
    o_conv = o_f + N_HEADS
    reps = LANES // N_HEADS
    w_f = jnp.tile(w_in[:, o_f:o_conv], (1, reps))
    w_all = jnp.concatenate([w_in[:, :ATTN_WIDTH] * (HEAD_DIM ** -0.5), w_in[:, ATTN_WIDTH:o_f],
                             w_in[:, o_conv:], w_f], axis=1).astype(BF16)
    bf = jnp.tile(b_forget, reps).reshape(1, LANES)

    qa, ka, va, z, sga, sgc = _inproj(h, row2(norm_mix), w_all, bf, *_bias_routing())
    attn = _attention(qa, ka, va)

    cw = jnp.concatenate([conv_w, jnp.zeros((CONV_HALO - CONV_K, CONV_WIDTH), F32)], axis=0)
    wr = jnp.concatenate([w_router, jnp.zeros((D, LANES - N_EXPERTS), F32)], axis=1).astype(BF16)
    br = jnp.concatenate([b_router, jnp.full((LANES - N_EXPERTS,), NEG_INF, F32)]).reshape(1, LANES)
    h1, xn2, logits = _mix(
        z, cw, row2(conv_b), row2(conv_ln_g), row2(conv_ln_b), attn,
        sga, sgc, h, w_attn_out.astype(BF16), w_conv_out.astype(BF16),
        w_o.astype(BF16), row2(norm_ffn), wr, br)

    rt, tile_cnt, tile_before, counts = _route(logits.reshape(T, LANES))
    cnt = counts[0, :N_EXPERTS].astype(jnp.int32)
    padded = ((cnt + MOE_BLK - 1) // MOE_BLK) * MOE_BLK
    pad_end = jnp.cumsum(padded)
    pad_start = pad_end - padded
    max_rows = T * TOP_K + (T // MOE_TILE) * N_EXPERTS * (RUN_ALIGN - 1)
    n_blocks = -(-max_rows // MOE_BLK) + N_EXPERTS
    n_rows = n_blocks * MOE_BLK
    n_used = pad_end[-1] // MOE_BLK
    run_len = tile_cnt[:, 0, :N_EXPERTS].astype(jnp.int32)
    run_off = jnp.cumsum(run_len, axis=1) - run_len
    run_dst = pad_start[None, :] + tile_before[:, 0, :N_EXPERTS].astype(jnp.int32)
    bits = jnp.asarray(_CHUNK_BITS, jnp.int32)
    nchunk = jnp.sum((run_len[:, :, None] >> bits) & 1, axis=1).astype(jnp.int32)
    runs = (run_dst.reshape(-1), run_len.reshape(-1), run_off.reshape(-1), nchunk.reshape(-1))
    blk = jnp.arange(n_blocks, dtype=jnp.int32)
    block_expert = jnp.minimum(
        jnp.sum((pad_end[None, :] <= (blk * MOE_BLK)[:, None]).astype(jnp.int32), axis=1), N_EXPERTS - 1)
    fill_rows = jnp.where(cnt > 0, pad_end - MOE_BLK, n_used * MOE_BLK).astype(jnp.int32)

    nused = n_used.astype(jnp.int32).reshape(1)
    x_pad = _dispatch(fill_rows, nused, *runs, rt, xn2.reshape(T, D), n_rows)
    y_pad = _experts(block_expert, nused, x_pad,
                     w_gate, b_gate.reshape(N_EXPERTS, 1, D), w_up,
                     b_up.reshape(N_EXPERTS, 1, D), w_down, b_down.reshape(N_EXPERTS, 1, D))
    return _tail, (*runs, y_pad, rt, h1.reshape(T, D),
                   p_l.reshape(T, PLE_DIM), row2(norm_ple), w_ple_gate.astype(BF16), w_ple_proj.astype(BF16))


def kernel(x, p, norm_mix, w_in, b_forget, w_attn_out, conv_w, conv_b, conv_ln_g, conv_ln_b, w_conv_out, w_o,
           norm_ffn, w_router, b_router, w_gate, b_gate, w_up, b_up, w_down, b_down, norm_ple, w_ple_gate,
           w_ple_proj, norm_final):
    B, S, D = x.shape
    assert p.shape[0] == 1 and D == D_MODEL
    tail, args = _layer(x, p[0], norm_mix[0], w_in[0], b_forget[0], w_attn_out[0], conv_w[0], conv_b[0],
                        conv_ln_g[0], conv_ln_b[0], w_conv_out[0], w_o[0], norm_ffn[0], w_router[0],
                        b_router[0], w_gate[0], b_gate[0], w_up[0], b_up[0], w_down[0], b_down[0],
                        norm_ple[0], w_ple_gate[0], w_ple_proj[0])
    return tail(*args, norm_final.reshape(1, D)).reshape(B, S, D)
```

```python
import functools

import jax
import jax.numpy as jnp
import numpy as np
from jax import lax
from jax.experimental import pallas as pl
from jax.experimental.pallas import tpu as pltpu

F32 = jnp.float32
BF16 = jnp.bfloat16

D_MODEL = 1024
HEAD_DIM = 64
N_HEADS = 8
ATTN_WIDTH = N_HEADS * HEAD_DIM
CONV_WIDTH = 512
CONV_K = 31
N_EXPERTS = 32
TOP_K = 4
SWIGLU_LIMIT = 7.0
SWIGLU_ALPHA = 1.702
PLE_DIM = 256
RMS_EPS = 1e-6
LN_EPS = 1e-5
NEG_INF = -1e30

LANES = 128
SUBLANES = 8
VMEM_LIMIT = 56 * 1024 * 1024

ROW_TILE = 512
ATTN_T = 512
HEAD_PAD = 128
CONV_HALO = 32
MOE_BLK = 512
MOE_TILE = 512
RUN_ALIGN = 16


def _cp(sem):
    return pltpu.CompilerParams(dimension_semantics=sem, vmem_limit_bytes=VMEM_LIMIT)


def _dot(a, b):
    return jnp.dot(a, b, preferred_element_type=F32)


def _sigmoid(x):
    return 1.0 / (1.0 + jnp.exp(-x))


def _rms(x, g):
    return x * lax.rsqrt(jnp.mean(x * x, axis=-1, keepdims=True) + RMS_EPS) * g


_C_QKV = 3 * ATTN_WIDTH
_C_CONV = _C_QKV + 2 * CONV_WIDTH
_C_GA = _C_CONV + D_MODEL
_C_GC = _C_GA + D_MODEL
_C_END = _C_GC + LANES


_HP_ALL = N_HEADS * HEAD_PAD
_N_CSPLIT = 3


def _split3(x):
    hi = x.astype(BF16)
    r1 = x - hi.astype(F32)
    mid = r1.astype(BF16)
    lo = (r1 - mid.astype(F32)).astype(BF16)
    return hi, mid, lo


def _store_heads(o_ref, src, aux):
    low = lax.broadcasted_iota(jnp.int32, (src.shape[0], LANES), 1) < HEAD_DIM
    for h in range(N_HEADS):
        chunk = src[:, (h // 2) * LANES:(h // 2 + 1) * LANES]
        if h % 2:
            chunk = pltpu.roll(chunk, HEAD_DIM, 1)
        o_ref[0, :, h * HEAD_PAD:(h + 1) * HEAD_PAD] = jnp.where(
            low, chunk, aux[:, h * HEAD_PAD:(h + 1) * HEAD_PAD]).astype(BF16)


def _inproj_kernel(x_ref, g_ref, w_ref, bf_ref, selq_ref, selk_ref, auxq_ref, auxk_ref, auxv_ref,
                   qa_ref, ka_ref, va_ref, z_ref, sga_ref, sgc_ref, carry_ref):
    tm = x_ref.shape[1]

    @pl.when(pl.program_id(1) == 0)
    def _():
        carry_ref[...] = jnp.zeros_like(carry_ref)

    xn = _rms(x_ref[0], g_ref[...]).astype(BF16)

    f = _dot(xn, w_ref[:, _C_GC:_C_END]) + bf_ref[...]
    lf = jnp.minimum(f, 0.0) - jnp.log(1.0 + jnp.exp(-jnp.abs(f)))
    r = lax.broadcasted_iota(jnp.int32, (tm, tm), 0)
    c = lax.broadcasted_iota(jnp.int32, (tm, tm), 1)
    tri = jnp.where(c <= r, 1.0, 0.0).astype(BF16)
    hi, mid, lo = _split3(lf)
    cs = _dot(tri, hi) + _dot(tri, mid) + _dot(tri, lo) + carry_ref[...]
    carry_ref[...] = cs[tm - 1:tm, :]
    chi, cmid, clo = (a.astype(F32) for a in _split3(cs))
    group = lax.broadcasted_iota(jnp.int32, (tm, LANES), 1) // N_HEADS
    csel = jnp.where(group == 0, chi, jnp.where(group == 1, cmid, jnp.where(
        group == 2, clo, jnp.where(group == 3, -chi, jnp.where(group == 4, -cmid, -clo))))).astype(BF16)
    aux_q = _dot(csel, selq_ref[...]) + auxq_ref[...]
    aux_k = _dot(csel, selk_ref[...]) + auxk_ref[...]

    qkv = _dot(xn, w_ref[:, 0:_C_QKV])
    _store_heads(qa_ref, qkv[:, 0:ATTN_WIDTH], aux_q)
    _store_heads(ka_ref, qkv[:, ATTN_WIDTH:2 * ATTN_WIDTH], aux_k)
    _store_heads(va_ref, qkv[:, 2 * ATTN_WIDTH:3 * ATTN_WIDTH], auxv_ref[...])
    cv = _dot(xn, w_ref[:, _C_QKV:_C_CONV])
    z_ref[0] = cv[:, 0:CONV_WIDTH] * _sigmoid(cv[:, CONV_WIDTH:])
    sga_ref[0] = _sigmoid(_dot(xn, w_ref[:, _C_CONV:_C_GA])).astype(BF16)
    sgc_ref[0] = _sigmoid(_dot(xn, w_ref[:, _C_GA:_C_GC])).astype(BF16)


def _inproj(x, g, w, bf, selq, selk, auxq, auxk, auxv):
    B, S, _ = x.shape
    tm = ROW_TILE
    row = lambda n: pl.BlockSpec((1, tm, n), lambda b, i: (b, i, 0))
    full = lambda a: pl.BlockSpec(a.shape, lambda b, i: (0,) * a.ndim, pipeline_mode=pl.Buffered(1))
    sds = lambda n, dt: jax.ShapeDtypeStruct((B, S, n), dt)
    return pl.pallas_call(
        _inproj_kernel,
        grid=(B, S // tm),
        in_specs=[row(D_MODEL)] + [full(a) for a in (g, w, bf, selq, selk, auxq, auxk, auxv)],
        out_specs=[row(_HP_ALL), row(_HP_ALL), row(_HP_ALL), row(CONV_WIDTH), row(D_MODEL), row(D_MODEL)],
        out_shape=[sds(_HP_ALL, BF16)] * 3 + [sds(CONV_WIDTH, F32), sds(D_MODEL, BF16), sds(D_MODEL, BF16)],
        scratch_shapes=[pltpu.VMEM((1, LANES), F32)],
        compiler_params=_cp(("parallel", "arbitrary")),
        name="inproj",
    )(x, g, w, bf, selq, selk, auxq, auxk, auxv)


def _bias_routing():
    selq = np.zeros((LANES, _HP_ALL), np.float32)
    selk = np.zeros((LANES, _HP_ALL), np.float32)
    auxq = np.zeros((1, _HP_ALL), np.float32)
    auxk = np.zeros((1, _HP_ALL), np.float32)
    auxv = np.zeros((1, _HP_ALL), np.float32)
    for h in range(N_HEADS):
        base = h * HEAD_PAD + HEAD_DIM
        for g in range(_N_CSPLIT):
            selq[g * N_HEADS + h, base + g] = 1.0
            auxk[0, base + g] = 1.0
            auxq[0, base + _N_CSPLIT + g] = 1.0
            selk[(_N_CSPLIT + g) * N_HEADS + h, base + _N_CSPLIT + g] = 1.0
        auxv[0, base] = 1.0
    return (jnp.asarray(selq, BF16), jnp.asarray(selk, BF16), jnp.asarray(auxq), jnp.asarray(auxk),
            jnp.asarray(auxv))


_HEADS_PER_STEP = 4


def _attn_kernel(qa_ref, ka_ref, va_ref, o_ref, s_ref):
    i = pl.program_id(2)
    t = ATTN_T
    dn = (((1,), (1,)), ((), ()))
    nc = t // LANES

    def scores(j, slot):
        r0 = pl.multiple_of(j * t, t)
        for hh in range(_HEADS_PER_STEP):
            lanes = slice(hh * HEAD_PAD, (hh + 1) * HEAD_PAD)
            s_ref[slot, hh] = lax.dot_general(qa_ref[0, :, lanes], ka_ref[0, pl.ds(r0, t), lanes], dn,
                                              preferred_element_type=F32)

    def consume(j, slot, carry, masked):
        r0 = pl.multiple_of(j * t, t)
        new = []
        for hh in range(_HEADS_PER_STEP):
            m, acc = carry[hh]
            lanes = slice(hh * HEAD_PAD, (hh + 1) * HEAD_PAD)
            s = s_ref[slot, hh]
            if masked:
                row = lax.broadcasted_iota(jnp.int32, (t, t), 0)
                col = lax.broadcasted_iota(jnp.int32, (t, t), 1)
                s = jnp.where(col <= row, s, NEG_INF)
            sc = [s[:, c * LANES:(c + 1) * LANES] for c in range(nc)]
            mloc = functools.reduce(jnp.maximum, sc)
            m_new = jnp.maximum(m, jnp.max(mloc, axis=-1, keepdims=True))
            p = jnp.concatenate([jnp.exp(x - m_new).astype(BF16) for x in sc], axis=1)
            acc = jnp.exp(m - m_new) * acc + _dot(p, va_ref[0, pl.ds(r0, t), lanes])
            new.append((m_new, acc))
        return tuple(new)

    def finish(carry):
        outs = [acc[:, 0:HEAD_DIM] / acc[:, HEAD_DIM:HEAD_DIM + 1] for _, acc in carry]
        o_ref[0] = jnp.concatenate(outs, axis=-1).astype(BF16)

    def pair(pp, carry):
        j = 2 * pp
        scores(j + 1, 1)
        carry = consume(j, 0, carry, False)
        scores(j + 2, 0)
        return consume(j + 1, 1, carry, False)

    init = tuple((jnp.full((t, LANES), NEG_INF, F32), jnp.zeros((t, HEAD_PAD), F32))
                 for _ in range(_HEADS_PER_STEP))
    scores(0, 0)
    carry = lax.fori_loop(0, i // 2, pair, init)

    @pl.when(i % 2 == 0)
    def _():
        finish(consume(i, 0, carry, True))

    @pl.when(i % 2 == 1)
    def _():
        scores(i, 1)
        finish(consume(i, 1, consume(i - 1, 0, carry, False), True))


def _attention(qa, ka, va):
    B, S, _ = qa.shape
    t = ATTN_T
    w = _HEADS_PER_STEP * HEAD_PAD
    return pl.pallas_call(
        _attn_kernel,
        grid=(B, N_HEADS // _HEADS_PER_STEP, S // t),
        in_specs=[pl.BlockSpec((1, t, w), lambda b, h, i: (b, i, h)),
                  pl.BlockSpec((1, S, w), lambda b, h, i: (b, 0, h)),
                  pl.BlockSpec((1, S, w), lambda b, h, i: (b, 0, h))],
        out_specs=pl.BlockSpec((1, t, _HEADS_PER_STEP * HEAD_DIM), lambda b, h, i: (b, i, h)),
        out_shape=jax.ShapeDtypeStruct((B, S, ATTN_WIDTH), BF16),
        scratch_shapes=[pltpu.VMEM((2, _HEADS_PER_STEP, t, t), F32)],
        compiler_params=_cp(("parallel", "parallel", "arbitrary")),
        name="fox_attention",
    )(qa, ka, va)


def _mix_kernel(z_ref, zprev_ref, cw_ref, cb_ref, lng_ref, lnb_ref, attn_ref, sga_ref, sgc_ref, x_ref,
                wa_ref, wc_ref, wo_ref, nf_ref, wr_ref, br_ref, h_ref, xn_ref, lg_ref, zs_ref, cv_ref, ph_ref):
    ts = z_ref.shape[1]
    i = pl.program_id(1)
    halo = zprev_ref[0]
    zs_ref[0:CONV_HALO, :] = jnp.where(i > 0, halo, 0.0)
    zs_ref[CONV_HALO:CONV_HALO + ts, :] = z_ref[0]
    base = CONV_HALO - (CONV_K - 1)
    for ph in range(SUBLANES):
        rows = ts + SUBLANES * ((CONV_K - 1 - ph) // SUBLANES)
        ph_ref[ph, 0:rows, :] = zs_ref[base + ph:base + ph + rows, :]
    rc = 128
    for c0 in range(0, CONV_WIDTH, LANES):
        for r0 in range(0, ts, rc):
            acc = jnp.zeros((rc, LANES), F32)
            for j in range(CONV_K):
                a0 = r0 + SUBLANES * (j // SUBLANES)
                acc = acc + cw_ref[j:j + 1, c0:c0 + LANES] * ph_ref[j % SUBLANES, a0:a0 + rc, c0:c0 + LANES]
            cv_ref[r0:r0 + rc, c0:c0 + LANES] = acc
    cv = cv_ref[...] + cb_ref[...]
    mu = jnp.mean(cv, axis=-1, keepdims=True)
    d = cv - mu
    var = jnp.mean(d * d, axis=-1, keepdims=True)
    y = d * lax.rsqrt(var + LN_EPS) * lng_ref[...] + lnb_ref[...]
    act = (y * _sigmoid(y)).astype(BF16)
    branch_c = _dot(act, wc_ref[...])
    branch_a = _dot(attn_ref[0], wa_ref[...])
    merged = sga_ref[0].astype(F32) * branch_a + sgc_ref[0].astype(F32) * branch_c
    h = x_ref[0] + _dot(merged.astype(BF16), wo_ref[...])
    h_ref[0] = h
    xn = _rms(h, nf_ref[...])
    xn_ref[0] = xn
    lg_ref[0] = _dot(xn.astype(BF16), wr_ref[...]) + br_ref[...]


def _mix(z, cw, cb, lng, lnb, attn, sga, sgc, x, wa, wc, wo, nf, wr, br):
    B, S, _ = x.shape
    ts = ROW_TILE
    hpb = ts // CONV_HALO
    row = lambda n: pl.BlockSpec((1, ts, n), lambda b, i: (b, i, 0))
    full = lambda a: pl.BlockSpec(a.shape, lambda b, i: (0,) * a.ndim)
    prev = pl.BlockSpec((1, CONV_HALO, CONV_WIDTH), lambda b, i: (b, jnp.maximum(i * hpb - 1, 0), 0))
    return pl.pallas_call(
        _mix_kernel,
        grid=(B, S // ts),
        in_specs=[row(CONV_WIDTH), prev, full(cw), full(cb), full(lng), full(lnb), row(ATTN_WIDTH),
                  row(D_MODEL), row(D_MODEL), row(D_MODEL), full(wa), full(wc), full(wo), full(nf),
                  full(wr), full(br)],
        out_specs=[row(D_MODEL), row(D_MODEL), row(LANES)],
        out_shape=[jax.ShapeDtypeStruct((B, S, D_MODEL), F32), jax.ShapeDtypeStruct((B, S, D_MODEL), F32),
                   jax.ShapeDtypeStruct((B, S, LANES), F32)],
        scratch_shapes=[pltpu.VMEM((CONV_HALO + ts, CONV_WIDTH), F32), pltpu.VMEM((ts, CONV_WIDTH), F32),
                        pltpu.VMEM((SUBLANES, ts + CONV_HALO - SUBLANES, CONV_WIDTH), F32)],
        compiler_params=_cp(("parallel", "parallel")),
        name="mix_merge",
    )(z, z, cw, cb, lng, lnb, attn, sga, sgc, x, wa, wc, wo, nf, wr, br)


def _route_kernel(lg_ref, o_ref, tcnt_ref, tbef_ref, cnt_ref, carry_ref):
    tr = lg_ref.shape[0]

    @pl.when(pl.program_id(0) == 0)
    def _():
        carry_ref[...] = jnp.zeros_like(carry_ref)

    lane = lax.broadcasted_iota(jnp.int32, (tr, LANES), 1).astype(F32)
    vals = lg_ref[...]
    top_v, top_i, hots = [], [], []
    for _ in range(TOP_K):
        m = jnp.max(vals, axis=-1, keepdims=True)
        idx = jnp.min(jnp.where(vals == m, lane, float(LANES)), axis=-1, keepdims=True)
        hot = lane == idx
        vals = jnp.where(hot, -jnp.inf, vals)
        top_v.append(m)
        top_i.append(idx)
        hots.append(hot)
    ex = [jnp.exp(v - top_v[0]) for v in top_v]
    den = ex[0] + ex[1] + ex[2] + ex[3]
    onehot = jnp.zeros((tr, LANES), F32)
    for hot in hots:
        onehot = onehot + jnp.where(hot, 1.0, 0.0)
    r = lax.broadcasted_iota(jnp.int32, (tr, tr), 0)
    c = lax.broadcasted_iota(jnp.int32, (tr, tr), 1)
    strict = jnp.where(c < r, 1.0, 0.0).astype(BF16)
    earlier = _dot(strict, onehot.astype(BF16))
    n_tile = jnp.sum(onehot, axis=0, keepdims=True)
    n_tile = jnp.floor((n_tile + (RUN_ALIGN - 1)) * (1.0 / RUN_ALIGN)) * RUN_ALIGN
    li = lax.broadcasted_iota(jnp.int32, (LANES, LANES), 0)
    lj = lax.broadcasted_iota(jnp.int32, (LANES, LANES), 1)
    lower = jnp.where(li < lj, 1.0, 0.0).astype(BF16)
    run_off = _dot(jnp.broadcast_to(n_tile, (8, LANES)).astype(BF16), lower)[0:1, :]
    within = earlier + run_off
    out = jnp.zeros((tr, LANES), F32)
    for k in range(TOP_K):
        pos = jnp.sum(jnp.where(hots[k], within, 0.0), axis=-1, keepdims=True)
        out = jnp.where(lane == k, top_i[k], out)
        out = jnp.where(lane == TOP_K + k, pos, out)
        out = jnp.where(lane == 2 * TOP_K + k, ex[k] / den, out)
    o_ref[...] = out
    tcnt_ref[0] = n_tile
    tbef_ref[0] = carry_ref[...]
    total = carry_ref[...] + n_tile
    carry_ref[...] = total
    cnt_ref[...] = total


def _route(lg):
    T = lg.shape[0]
    tr = MOE_TILE
    per_tile = pl.BlockSpec((1, 1, LANES), lambda i: (i, 0, 0))
    return pl.pallas_call(
        _route_kernel,
        grid=(T // tr,),
        in_specs=[pl.BlockSpec((tr, LANES), lambda i: (i, 0))],
        out_specs=[pl.BlockSpec((tr, LANES), lambda i: (i, 0)), per_tile, per_tile,
                   pl.BlockSpec((1, LANES), lambda i: (0, 0))],
        out_shape=[jax.ShapeDtypeStruct((T, LANES), F32), jax.ShapeDtypeStruct((T // tr, 1, LANES), F32),
                   jax.ShapeDtypeStruct((T // tr, 1, LANES), F32), jax.ShapeDtypeStruct((1, LANES), F32)],
        scratch_shapes=[pltpu.VMEM((1, LANES), F32)],
        compiler_params=_cp(("arbitrary",)),
        name="route_topk",
    )(lg)


_ALIGN_BITS = RUN_ALIGN.bit_length() - 1
_TILE_ROWS = MOE_TILE * TOP_K + N_EXPERTS * RUN_ALIGN
_CHUNK_BITS = tuple(range(_ALIGN_BITS, MOE_TILE.bit_length()))
_N_CHUNK = len(_CHUNK_BITS)


def _run_copies(tile, slot, run_dst_ref, run_len_ref, run_off_ref, tile_buf, hbm_ref, sems, to_hbm):
    def one_expert(e, c):
        idx = tile * N_EXPERTS + e
        n, dst, off = run_len_ref[idx], run_dst_ref[idx], run_off_ref[idx]
        for s, b in enumerate(_CHUNK_BITS):
            size = 1 << b

            @pl.when((n & size) != 0)
            def _():
                done = (n >> (b + 1)) << (b + 1)
                loc = tile_buf.at[slot, pl.ds(pl.multiple_of(off + done, RUN_ALIGN), size)]
                far = hbm_ref.at[pl.ds(pl.multiple_of(dst + done, RUN_ALIGN), size)]
                src, dstn = (loc, far) if to_hbm else (far, loc)
                pltpu.make_async_copy(src, dstn, sems.at[slot, s]).start(priority=s % 2)
        return c

    lax.fori_loop(0, N_EXPERTS, one_expert, 0)


def _run_waits(tile, slot, nchunk_ref, tile_buf, hbm_ref, sems, to_hbm):
    for s, b in enumerate(_CHUNK_BITS):
        size = 1 << b
        loc = tile_buf.at[slot, pl.ds(0, size)]
        far = hbm_ref.at[pl.ds(0, size)]
        src, dstn = (loc, far) if to_hbm else (far, loc)
        cp = pltpu.make_async_copy(src, dstn, sems.at[slot, s])
        lax.fori_loop(0, nchunk_ref[tile * _N_CHUNK + s], lambda n, c: (cp.wait(), c)[1], 0)


def _dispatch_kernel(tail_ref, nused_ref, run_dst_ref, run_len_ref, run_off_ref, nchunk_ref,
                     rt_ref, x_ref, o_ref, xs_ref, zero_ref, sems, fill_sem, trail_sem):
    i = pl.program_id(0)
    slot = i % 2
    n_blocks = o_ref.shape[0] // MOE_BLK

    def fill(row, sem):
        return pltpu.make_async_copy(zero_ref, o_ref.at[pl.ds(pl.multiple_of(row, MOE_BLK), MOE_BLK)], sem)

    def trailing(op):
        lax.fori_loop(nused_ref[0], n_blocks, lambda n, c: (op(fill(n * MOE_BLK, trail_sem)), c)[1], 0)

    @pl.when(i == 0)
    def _():
        zero_ref[...] = jnp.zeros_like(zero_ref)
        lax.fori_loop(0, N_EXPERTS, lambda n, c: (fill(tail_ref[n], fill_sem).start(), c)[1], 0)
        trailing(lambda cp: cp.start())
        lax.fori_loop(0, N_EXPERTS, lambda n, c: (fill(tail_ref[n], fill_sem).wait(), c)[1], 0)

    pos_t = rt_ref[...].T
    xb = x_ref[...].astype(BF16)
    for r0 in range(0, _TILE_ROWS, MOE_TILE):
        rows = (lax.broadcasted_iota(jnp.int32, (MOE_TILE, MOE_TILE), 0) + r0).astype(F32)
        place = jnp.zeros((MOE_TILE, MOE_TILE), F32)
        for k in range(TOP_K):
            place = jnp.where(rows == pos_t[TOP_K + k:TOP_K + k + 1, :], 1.0, place)
        xs_ref[slot, r0:r0 + MOE_TILE, :] = _dot(place.astype(BF16), xb).astype(BF16)

    _run_copies(i, slot, run_dst_ref, run_len_ref, run_off_ref, xs_ref, o_ref, sems, True)

    @pl.when(i > 0)
    def _():
        _run_waits(i - 1, 1 - slot, nchunk_ref, xs_ref, o_ref, sems, True)

    @pl.when(i == pl.num_programs(0) - 1)
    def _():
        _run_waits(i, slot, nchunk_ref, xs_ref, o_ref, sems, True)
        trailing(lambda cp: cp.wait())


def _dispatch(tail_rows, nused, run_dst, run_len, run_off, nchunk, rt, xn, n_rows):
    T = xn.shape[0]
    tb = MOE_TILE
    return pl.pallas_call(
        _dispatch_kernel,
        grid_spec=pltpu.PrefetchScalarGridSpec(
            num_scalar_prefetch=6,
            grid=(T // tb,),
            in_specs=[pl.BlockSpec((tb, LANES), lambda i, *_: (i, 0)),
                      pl.BlockSpec((tb, D_MODEL), lambda i, *_: (i, 0))],
            out_specs=pl.BlockSpec(memory_space=pl.ANY),
            scratch_shapes=[pltpu.VMEM((2, _TILE_ROWS, D_MODEL), BF16), pltpu.VMEM((MOE_BLK, D_MODEL), BF16),
                            pltpu.SemaphoreType.DMA((2, _N_CHUNK)), pltpu.SemaphoreType.DMA(()),
                            pltpu.SemaphoreType.DMA(())],
        ),
        out_shape=jax.ShapeDtypeStruct((n_rows, D_MODEL), BF16),
        compiler_params=_cp(("arbitrary",)),
        name="moe_dispatch",
    )(tail_rows, nused, run_dst, run_len, run_off, nchunk, rt, xn)


def _expert_kernel(be_ref, nused_ref, x_ref, wg_ref, bg_ref, wu_ref, bu_ref, wd_ref, bd_ref, y_ref,
                   wgb_ref, wub_ref, wdb_ref):
    i = pl.program_id(0)

    @pl.when((i == 0) | (be_ref[i] != be_ref[jnp.maximum(i - 1, 0)]))
    def _():
        wgb_ref[...] = wg_ref[0].astype(BF16)
        wub_ref[...] = wu_ref[0].astype(BF16)
        wdb_ref[...] = wd_ref[0].astype(BF16)

    @pl.when(i < nused_ref[0])
    def _():
        xb = x_ref[...]
        g = _dot(xb, wgb_ref[...]) + bg_ref[0]
        u = _dot(xb, wub_ref[...]) + bu_ref[0]
        g = jnp.minimum(g, SWIGLU_LIMIT)
        u = jnp.clip(u, -SWIGLU_LIMIT, SWIGLU_LIMIT)
        glu = g * _sigmoid(SWIGLU_ALPHA * g)
        y_ref[...] = (_dot(((u + 1.0) * glu).astype(BF16), wdb_ref[...]) + bd_ref[0]).astype(BF16)

    @pl.when(i >= nused_ref[0])
    def _():
        y_ref[...] = jnp.zeros_like(y_ref)


def _experts(block_expert, nused, x_pad, wg, bg, wu, bu, wd, bd):
    n_rows = x_pad.shape[0]
    wspec = pl.BlockSpec((1, D_MODEL, D_MODEL), lambda i, be, nu: (be[i], 0, 0))
    bspec = pl.BlockSpec((1, 1, D_MODEL), lambda i, be, nu: (be[i], 0, 0))
    yspec = pl.BlockSpec((MOE_BLK, D_MODEL), lambda i, be, nu: (i, 0))
    xspec = pl.BlockSpec((MOE_BLK, D_MODEL), lambda i, be, nu: (jnp.minimum(i, nu[0] - 1), 0))
    return pl.pallas_call(
        _expert_kernel,
        grid_spec=pltpu.PrefetchScalarGridSpec(
            num_scalar_prefetch=2,
            grid=(n_rows // MOE_BLK,),
            in_specs=[xspec, wspec, bspec, wspec, bspec, wspec, bspec],
            out_specs=yspec,
            scratch_shapes=[pltpu.VMEM((D_MODEL, D_MODEL), BF16)] * 3,
        ),
        out_shape=jax.ShapeDtypeStruct((n_rows, D_MODEL), BF16),
        compiler_params=_cp(("arbitrary",)),
        name="moe_experts",
    )(block_expert, nused, x_pad, wg, bg, wu, bu, wd, bd)


def _tail_kernel(run_dst_ref, run_len_ref, run_off_ref, nchunk_ref, y_ref, rt_ref, h_ref, p_ref, npl_ref,
                 wpg_ref, wpp_ref, nfin_ref, o_ref, ys_ref, sems):
    i = pl.program_id(0)
    slot = i % 2
    fetch = functools.partial(_run_copies, run_dst_ref=run_dst_ref, run_len_ref=run_len_ref,
                              run_off_ref=run_off_ref, tile_buf=ys_ref, hbm_ref=y_ref, sems=sems, to_hbm=False)

    @pl.when(i == 0)
    def _():
        ys_ref[...] = jnp.zeros_like(ys_ref)
        fetch(i, slot)

    @pl.when(i + 1 < pl.num_programs(0))
    def _():
        fetch(i + 1, 1 - slot)

    _run_waits(i, slot, nchunk_ref, ys_ref, y_ref, sems, False)

    rt = rt_ref[...]
    cols = lax.broadcasted_iota(jnp.int32, (MOE_TILE, _TILE_ROWS), 1).astype(F32)
    weight = jnp.zeros((MOE_TILE, _TILE_ROWS), F32)
    for k in range(TOP_K):
        weight = jnp.where(cols == rt[:, TOP_K + k:TOP_K + k + 1],
                           rt[:, 2 * TOP_K + k:2 * TOP_K + k + 1], weight)
    h = h_ref[...] + _dot(weight.astype(BF16), ys_ref[slot])
    pg = _sigmoid(_dot(_rms(h, npl_ref[...]).astype(BF16), wpg_ref[...]))
    h = h + pg * _dot(p_ref[...].astype(BF16), wpp_ref[...])
    o_ref[...] = _rms(h, nfin_ref[...])


def _tail(run_dst, run_len, run_off, nchunk, y_pad, rt, h1, p2, npl, wpg, wpp, nfin):
    T = h1.shape[0]
    tc = MOE_TILE
    row = lambda n: pl.BlockSpec((tc, n), lambda i, *_: (i, 0))
    full = lambda a: pl.BlockSpec(a.shape, lambda i, *_: (0,) * a.ndim)
    return pl.pallas_call(
        _tail_kernel,
        grid_spec=pltpu.PrefetchScalarGridSpec(
            num_scalar_prefetch=4,
            grid=(T // tc,),
            in_specs=[pl.BlockSpec(memory_space=pl.ANY), row(LANES), row(D_MODEL), row(PLE_DIM),
                      full(npl), full(wpg), full(wpp), full(nfin)],
            out_specs=row(D_MODEL),
            scratch_shapes=[pltpu.VMEM((2, _TILE_ROWS, D_MODEL), BF16), pltpu.SemaphoreType.DMA((2, _N_CHUNK))],
        ),
        out_shape=jax.ShapeDtypeStruct((T, D_MODEL), F32),
        compiler_params=_cp(("arbitrary",)),
        name="moe_combine_tail",
    )(run_dst, run_len, run_off, nchunk, y_pad, rt, h1, p2, npl, wpg, wpp, nfin)


def _layer(h, p_l, norm_mix, w_in, b_forget, w_attn_out, conv_w, conv_b, conv_ln_g, conv_ln_b, w_conv_out, w_o,
           norm_ffn, w_router, b_router, w_gate, b_gate, w_up, b_up, w_down, b_down, norm_ple, w_ple_gate,
           w_ple_proj):
    B, S, D = h.shape
    T = B * S
    row2 = lambda a: a.reshape(1, -1)

    o_f = 3 * ATTN_WIDTH
    o_conv = o_f + N_HEADS
    reps = LANES // N_HEADS
    w_f = jnp.tile(w_in[:, o_f:o_conv], (1, reps))
    w_all = jnp.concatenate([w_in[:, :ATTN_WIDTH] * (HEAD_DIM ** -0.5), w_in[:, ATTN_WIDTH:o_f],
                             w_in[:, o_conv:], w_f], axis=1).astype(BF16)
    bf = jnp.tile(b_forget, reps).reshape(1, LANES)

    qa, ka, va, z, sga, sgc = _inproj(h, row2(norm_mix), w_all, bf, *_bias_routing())
    attn = _attention(qa, ka, va)

    cw = jnp.concatenate([conv_w, jnp.zeros((CONV_HALO - CONV_K, CONV_WIDTH), F32)], axis=0)
    wr = jnp.concatenate([w_router, jnp.zeros((D, LANES - N_EXPERTS), F32)], axis=1).astype(BF16)
    br = jnp.concatenate([b_router, jnp.full((LANES - N_EXPERTS,), NEG_INF, F32)]).reshape(1, LANES)
    h1, xn2, logits = _mix(
        z, cw, row2(conv_b), row2(conv_ln_g), row2(conv_ln_b), attn,
        sga, sgc, h, w_attn_out.astype(BF16), w_conv_out.astype(BF16),
        w_o.astype(BF16), row2(norm_ffn), wr, br)

    rt, tile_cnt, tile_before, counts = _route(logits.reshape(T, LANES))
    cnt = counts[0, :N_EXPERTS].astype(jnp.int32)
    padded = ((cnt + MOE_BLK - 1) // MOE_BLK) * MOE_BLK
    pad_end = jnp.cumsum(padded)
    pad_start = pad_end - padded
    max_rows = T * TOP_K + (T // MOE_TILE) * N_EXPERTS * (RUN_ALIGN - 1)
    n_blocks = -(-max_rows // MOE_BLK) + N_EXPERTS
    n_rows = n_blocks * MOE_BLK
    n_used = pad_end[-1] // MOE_BLK
    run_len = tile_cnt[:, 0, :N_EXPERTS].astype(jnp.int32)
    run_off = jnp.cumsum(run_len, axis=1) - run_len
    run_dst = pad_start[None, :] + tile_before[:, 0, :N_EXPERTS].astype(jnp.int32)
    bits = jnp.asarray(_CHUNK_BITS, jnp.int32)
    nchunk = jnp.sum((run_len[:, :, None] >> bits) & 1, axis=1).astype(jnp.int32)
    runs = (run_dst.reshape(-1), run_len.reshape(-1), run_off.reshape(-1), nchunk.reshape(-1))
    blk = jnp.arange(n_blocks, dtype=jnp.int32)
    block_expert = jnp.minimum(
        jnp.sum((pad_end[None, :] <= (blk * MOE_BLK)[:, None]).astype(jnp.int32), axis=1), N_EXPERTS - 1)
    fill_rows = jnp.where(cnt > 0, pad_end - MOE_BLK, n_used * MOE_BLK).astype(jnp.int32)

    nused = n_used.astype(jnp.int32).reshape(1)
    x_pad = _dispatch(fill_rows, nused, *runs, rt, xn2.reshape(T, D), n_rows)
    y_pad = _experts(block_expert, nused, x_pad,
                     w_gate, b_gate.reshape(N_EXPERTS, 1, D), w_up,
                     b_up.reshape(N_EXPERTS, 1, D), w_down, b_down.reshape(N_EXPERTS, 1, D))
    return _tail, (*runs, y_pad, rt, h1.reshape(T, D),
                   p_l.reshape(T, PLE_DIM), row2(norm_ple), w_ple_gate.astype(BF16), w_ple_proj.astype(BF16))


def kernel(x, p, norm_mix, w_in, b_forget, w_attn_out, conv_w, conv_b, conv_ln_g, conv_ln_b, w_conv_out, w_o,
           norm_ffn, w_router, b_router, w_gate, b_gate, w_up, b_up, w_down, b_down, norm_ple, w_ple_gate,
           w_ple_proj, norm_final):
    B, S, D = x.shape
    assert p.shape[0] == 1 and D == D_MODEL
    tail, args = _layer(x, p[0], norm_mix[0], w_in[0], b_forget[0], w_attn_out[0], conv_w[0], conv_b[0],
                        conv_ln_g[0], conv_ln_b[0], w_conv_out[0], w_o[0], norm_ffn[0], w_router[0],
                        b_router[0], w_gate[0], b_gate[0], w_up[0], b_up[0], w_down[0], b_down[0],
                        norm_ple[0], w_ple_gate[0], w_ple_proj[0])
    return tail(*args, norm_final.reshape(1, D)).reshape(B, S, D)
```

```python
import functools

import jax
import jax.numpy as jnp
import numpy as np
from jax import lax
from jax.experimental import pallas as pl
from jax.experimental.pallas import tpu as pltpu

F32 = jnp.float32
BF16 = jnp.bfloat16

D_MODEL = 1024
HEAD_DIM = 64
N_HEADS = 8
ATTN_WIDTH = N_HEADS * HEAD_DIM
CONV_WIDTH = 512
CONV_K = 31
N_EXPERTS = 32
TOP_K = 4
SWIGLU_LIMIT = 7.0
SWIGLU_ALPHA = 1.702
PLE_DIM = 256
RMS_EPS = 1e-6
LN_EPS = 1e-5
NEG_INF = -1e30

LANES = 128
SUBLANES = 8
VMEM_LIMIT = 56 * 1024 * 1024

ROW_TILE = 512
ATTN_T = 512
HEAD_PAD = 128
CONV_HALO = 32
MOE_BLK = 512
MOE_TILE = 512
RUN_ALIGN = 16


def _cp(sem):
    return pltpu.CompilerParams(dimension_semantics=sem, vmem_limit_bytes=VMEM_LIMIT)


def _dot(a, b):
    return jnp.dot(a, b, preferred_element_type=F32)


def _sigmoid(x):
    return 1.0 / (1.0 + jnp.exp(-x))


def _rms(x, g):
    return x * lax.rsqrt(jnp.mean(x * x, axis=-1, keepdims=True) + RMS_EPS) * g


_C_QKV = 3 * ATTN_WIDTH
_C_CONV = _C_QKV + 2 * CONV_WIDTH
_C_GA = _C_CONV + D_MODEL
_C_GC = _C_GA + D_MODEL
_C_END = _C_GC + LANES


_HP_ALL = N_HEADS * HEAD_PAD
_N_CSPLIT = 3


def _split3(x):
    hi = x.astype(BF16)
    r1 = x - hi.astype(F32)
    mid = r1.astype(BF16)
    lo = (r1 - mid.astype(F32)).astype(BF16)
    return hi, mid, lo


def _store_heads(o_ref, src, aux):
    low = lax.broadcasted_iota(jnp.int32, (src.shape[0], LANES), 1) < HEAD_DIM
    for h in range(N_HEADS):
        chunk = src[:, (h // 2) * LANES:(h // 2 + 1) * LANES]
        if h % 2:
            chunk = pltpu.roll(chunk, HEAD_DIM, 1)
        o_ref[0, :, h * HEAD_PAD:(h + 1) * HEAD_PAD] = jnp.where(
            low, chunk, aux[:, h * HEAD_PAD:(h + 1) * HEAD_PAD]).astype(BF16)


def _inproj_kernel(x_ref, g_ref, w_ref, bf_ref, selq_ref, selk_ref, auxq_ref, auxk_ref, auxv_ref,
                   qa_ref, ka_ref, va_ref, z_ref, sga_ref, sgc_ref, carry_ref):
    tm = x_ref.shape[1]

    @pl.when(pl.program_id(1) == 0)
    def _():
        carry_ref[...] = jnp.zeros_like(carry_ref)

    xn = _rms(x_ref[0], g_ref[...]).astype(BF16)

    f = _dot(xn, w_ref[:, _C_GC:_C_END]) + bf_ref[...]
    lf = jnp.minimum(f, 0.0) - jnp.log(1.0 + jnp.exp(-jnp.abs(f)))
    r = lax.broadcasted_iota(jnp.int32, (tm, tm), 0)
    c = lax.broadcasted_iota(jnp.int32, (tm, tm), 1)
    tri = jnp.where(c <= r, 1.0, 0.0).astype(BF16)
    hi, mid, lo = _split3(lf)
    cs = _dot(tri, hi) + _dot(tri, mid) + _dot(tri, lo) + carry_ref[...]
    carry_ref[...] = cs[tm - 1:tm, :]
    chi, cmid, clo = (a.astype(F32) for a in _split3(cs))
    group = lax.broadcasted_iota(jnp.int32, (tm, LANES), 1) // N_HEADS
    csel = jnp.where(group == 0, chi, jnp.where(group == 1, cmid, jnp.where(
        group == 2, clo, jnp.where(group == 3, -chi, jnp.where(group == 4, -cmid, -clo))))).astype(BF16)
    aux_q = _dot(csel, selq_ref[...]) + auxq_ref[...]
    aux_k = _dot(csel, selk_ref[...]) + auxk_ref[...]

    qkv = _dot(xn, w_ref[:, 0:_C_QKV])
    _store_heads(qa_ref, qkv[:, 0:ATTN_WIDTH], aux_q)
    _store_heads(ka_ref, qkv[:, ATTN_WIDTH:2 * ATTN_WIDTH], aux_k)
    _store_heads(va_ref, qkv[:, 2 * ATTN_WIDTH:3 * ATTN_WIDTH], auxv_ref[...])
    cv = _dot(xn, w_ref[:, _C_QKV:_C_CONV])
    z_ref[0] = cv[:, 0:CONV_WIDTH] * _sigmoid(cv[:, CONV_WIDTH:])
    sga_ref[0] = _sigmoid(_dot(xn, w_ref[:, _C_CONV:_C_GA])).astype(BF16)
    sgc_ref[0] = _sigmoid(_dot(xn, w_ref[:, _C_GA:_C_GC])).astype(BF16)


def _inproj(x, g, w, bf, selq, selk, auxq, auxk, auxv):
    B, S, _ = x.shape
    tm = ROW_TILE
    row = lambda n: pl.BlockSpec((1, tm, n), lambda b, i: (b, i, 0))
    full = lambda a: pl.BlockSpec(a.shape, lambda b, i: (0,) * a.ndim, pipeline_mode=pl.Buffered(1))
    sds = lambda n, dt: jax.ShapeDtypeStruct((B, S, n), dt)
    return pl.pallas_call(
        _inproj_kernel,
        grid=(B, S // tm),
        in_specs=[row(D_MODEL)] + [full(a) for a in (g, w, bf, selq, selk, auxq, auxk, auxv)],
        out_specs=[row(_HP_ALL), row(_HP_ALL), row(_HP_ALL), row(CONV_WIDTH), row(D_MODEL), row(D_MODEL)],
        out_shape=[sds(_HP_ALL, BF16)] * 3 + [sds(CONV_WIDTH, F32), sds(D_MODEL, BF16), sds(D_MODEL, BF16)],
        scratch_shapes=[pltpu.VMEM((1, LANES), F32)],
        compiler_params=_cp(("parallel", "arbitrary")),
        name="inproj",
    )(x, g, w, bf, selq, selk, auxq, auxk, auxv)


def _bias_routing():
    selq = np.zeros((LANES, _HP_ALL), np.float32)
    selk = np.zeros((LANES, _HP_ALL), np.float32)
    auxq = np.zeros((1, _HP_ALL), np.float32)
    auxk = np.zeros((1, _HP_ALL), np.float32)
    auxv = np.zeros((1, _HP_ALL), np.float32)
    for h in range(N_HEADS):
        base = h * HEAD_PAD + HEAD_DIM
        for g in range(_N_CSPLIT):
            selq[g * N_HEADS + h, base + g] = 1.0
            auxk[0, base + g] = 1.0
            auxq[0, base + _N_CSPLIT + g] = 1.0
            selk[(_N_CSPLIT + g) * N_HEADS + h, base + _N_CSPLIT + g] = 1.0
        auxv[0, base] = 1.0
    return (jnp.asarray(selq, BF16), jnp.asarray(selk, BF16), jnp.asarray(auxq), jnp.asarray(auxk),
            jnp.asarray(auxv))


_HEADS_PER_STEP = 4


def _attn_kernel(qa_ref, ka_ref, va_ref, o_ref, s_ref):
    i = pl.program_id(2)
    t = ATTN_T
    dn = (((1,), (1,)), ((), ()))
    nc = t // LANES

    def scores(j, slot):
        r0 = pl.multiple_of(j * t, t)
        for hh in range(_HEADS_PER_STEP):
            lanes = slice(hh * HEAD_PAD, (hh + 1) * HEAD_PAD)
            s_ref[slot, hh] = lax.dot_general(qa_ref[0, :, lanes], ka_ref[0, pl.ds(r0, t), lanes], dn,
                                              preferred_element_type=F32)

    def consume(j, slot, carry, masked):
        r0 = pl.multiple_of(j * t, t)
        new = []
        for hh in range(_HEADS_PER_STEP):
            m, acc = carry[hh]
            lanes = slice(hh * HEAD_PAD, (hh + 1) * HEAD_PAD)
            s = s_ref[slot, hh]
            if masked:
                row = lax.broadcasted_iota(jnp.int32, (t, t), 0)
                col = lax.broadcasted_iota(jnp.int32, (t, t), 1)
                s = jnp.where(col <= row, s, NEG_INF)
            sc = [s[:, c * LANES:(c + 1) * LANES] for c in range(nc)]
            mloc = functools.reduce(jnp.maximum, sc)
            m_new = jnp.maximum(m, jnp.max(mloc, axis=-1, keepdims=True))
            p = jnp.concatenate([jnp.exp(x - m_new).astype(BF16) for x in sc], axis=1)
            acc = jnp.exp(m - m_new) * acc + _dot(p, va_ref[0, pl.ds(r0, t), lanes])
            new.append((m_new, acc))
        return tuple(new)

    def finish(carry):
        outs = [acc[:, 0:HEAD_DIM] / acc[:, HEAD_DIM:HEAD_DIM + 1] for _, acc in carry]
        o_ref[0] = jnp.concatenate(outs, axis=-1).astype(BF16)

    def pair(pp, carry):
        j = 2 * pp
        scores(j + 1, 1)
        carry = consume(j, 0, carry, False)
        scores(j + 2, 0)
        return consume(j + 1, 1, carry, False)

    init = tuple((jnp.full((t, LANES), NEG_INF, F32), jnp.zeros((t, HEAD_PAD), F32))
                 for _ in range(_HEADS_PER_STEP))
    scores(0, 0)
    carry = lax.fori_loop(0, i // 2, pair, init)

    @pl.when(i % 2 == 0)
    def _():
        finish(consume(i, 0, carry, True))

    @pl.when(i % 2 == 1)
    def _():
        scores(i, 1)
        finish(consume(i, 1, consume(i - 1, 0, carry, False), True))


def _attention(qa, ka, va):
    B, S, _ = qa.shape
    t = ATTN_T
    w = _HEADS_PER_STEP * HEAD_PAD
    return pl.pallas_call(
        _attn_kernel,
        grid=(B, N_HEADS // _HEADS_PER_STEP, S // t),
        in_specs=[pl.BlockSpec((1, t, w), lambda b, h, i: (b, i, h)),
                  pl.BlockSpec((1, S, w), lambda b, h, i: (b, 0, h)),
                  pl.BlockSpec((1, S, w), lambda b, h, i: (b, 0, h))],
        out_specs=pl.BlockSpec((1, t, _HEADS_PER_STEP * HEAD_DIM), lambda b, h, i: (b, i, h)),
        out_shape=jax.ShapeDtypeStruct((B, S, ATTN_WIDTH), BF16),
        scratch_shapes=[pltpu.VMEM((2, _HEADS_PER_STEP, t, t), F32)],
        compiler_params=_cp(("parallel", "parallel", "arbitrary")),
        name="fox_attention",
    )(qa, ka, va)


def _mix_kernel(z_ref, zprev_ref, cw_ref, cb_ref, lng_ref, lnb_ref, attn_ref, sga_ref, sgc_ref, x_ref,
                wa_ref, wc_ref, wo_ref, nf_ref, wr_ref, br_ref, h_ref, xn_ref, lg_ref, zs_ref, cv_ref, ph_ref):
    ts = z_ref.shape[1]
    i = pl.program_id(1)
    halo = zprev_ref[0]
    zs_ref[0:CONV_HALO, :] = jnp.where(i > 0, halo, 0.0)
    zs_ref[CONV_HALO:CONV_HALO + ts, :] = z_ref[0]
    base = CONV_HALO - (CONV_K - 1)
    for ph in range(SUBLANES):
        rows = ts + SUBLANES * ((CONV_K - 1 - ph) // SUBLANES)
        ph_ref[ph, 0:rows, :] = zs_ref[base + ph:base + ph + rows, :]
    rc = 128
    for c0 in range(0, CONV_WIDTH, LANES):
        for r0 in range(0, ts, rc):
            acc = jnp.zeros((rc, LANES), F32)
            for j in range(CONV_K):
                a0 = r0 + SUBLANES * (j // SUBLANES)
                acc = acc + cw_ref[j:j + 1, c0:c0 + LANES] * ph_ref[j % SUBLANES, a0:a0 + rc, c0:c0 + LANES]
            cv_ref[r0:r0 + rc, c0:c0 + LANES] = acc
    cv = cv_ref[...] + cb_ref[...]
    mu = jnp.mean(cv, axis=-1, keepdims=True)
    d = cv - mu
    var = jnp.mean(d * d, axis=-1, keepdims=True)
    y = d * lax.rsqrt(var + LN_EPS) * lng_ref[...] + lnb_ref[...]
    act = (y * _sigmoid(y)).astype(BF16)
    branch_c = _dot(act, wc_ref[...])
    branch_a = _dot(attn_ref[0], wa_ref[...])
    merged = sga_ref[0].astype(F32) * branch_a + sgc_ref[0].astype(F32) * branch_c
    h = x_ref[0] + _dot(merged.astype(BF16), wo_ref[...])
    h_ref[0] = h
    xn = _rms(h, nf_ref[...]).astype(BF16)
    xn_ref[0] = xn
    lg_ref[0] = _dot(xn, wr_ref[...]) + br_ref[...]


def _mix(z, cw, cb, lng, lnb, attn, sga, sgc, x, wa, wc, wo, nf, wr, br):
    B, S, _ = x.shape
    ts = ROW_TILE
    hpb = ts // CONV_HALO
    row = lambda n: pl.BlockSpec((1, ts, n), lambda b, i: (b, i, 0))
    full = lambda a: pl.BlockSpec(a.shape, lambda b, i: (0,) * a.ndim)
    prev = pl.BlockSpec((1, CONV_HALO, CONV_WIDTH), lambda b, i: (b, jnp.maximum(i * hpb - 1, 0), 0))
    return pl.pallas_call(
        _mix_kernel,
        grid=(B, S // ts),
        in_specs=[row(CONV_WIDTH), prev, full(cw), full(cb), full(lng), full(lnb), row(ATTN_WIDTH),
                  row(D_MODEL), row(D_MODEL), row(D_MODEL), full(wa), full(wc), full(wo), full(nf),
                  full(wr), full(br)],
        out_specs=[row(D_MODEL), row(D_MODEL), row(LANES)],
        out_shape=[jax.ShapeDtypeStruct((B, S, D_MODEL), F32), jax.ShapeDtypeStruct((B, S, D_MODEL), BF16),
                   jax.ShapeDtypeStruct((B, S, LANES), F32)],
        scratch_shapes=[pltpu.VMEM((CONV_HALO + ts, CONV_WIDTH), F32), pltpu.VMEM((ts, CONV_WIDTH), F32),
                        pltpu.VMEM((SUBLANES, ts + CONV_HALO - SUBLANES, CONV_WIDTH), F32)],
        compiler_params=_cp(("parallel", "parallel")),
        name="mix_merge",
    )(z, z, cw, cb, lng, lnb, attn, sga, sgc, x, wa, wc, wo, nf, wr, br)


def _route_kernel(lg_ref, o_ref, tcnt_ref, tbef_ref, cnt_ref, carry_ref):
    tr = lg_ref.shape[0]

    @pl.when(pl.program_id(0) == 0)
    def _():
        carry_ref[...] = jnp.zeros_like(carry_ref)

    lane = lax.broadcasted_iota(jnp.int32, (tr, LANES), 1).astype(F32)
    vals = lg_ref[...]
    top_v, top_i, hots = [], [], []
    for _ in range(TOP_K):
        m = jnp.max(vals, axis=-1, keepdims=True)
        idx = jnp.min(jnp.where(vals == m, lane, float(LANES)), axis=-1, keepdims=True)
        hot = lane == idx
        vals = jnp.where(hot, -jnp.inf, vals)
        top_v.append(m)
        top_i.append(idx)
        hots.append(hot)
    ex = [jnp.exp(v - top_v[0]) for v in top_v]
    den = ex[0] + ex[1] + ex[2] + ex[3]
    onehot = jnp.zeros((tr, LANES), F32)
    for hot in hots:
        onehot = onehot + jnp.where(hot, 1.0, 0.0)
    r = lax.broadcasted_iota(jnp.int32, (tr, tr), 0)
    c = lax.broadcasted_iota(jnp.int32, (tr, tr), 1)
    strict = jnp.where(c < r, 1.0, 0.0).astype(BF16)
    earlier = _dot(strict, onehot.astype(BF16))
    n_tile = jnp.sum(onehot, axis=0, keepdims=True)
    n_tile = jnp.floor((n_tile + (RUN_ALIGN - 1)) * (1.0 / RUN_ALIGN)) * RUN_ALIGN
    li = lax.broadcasted_iota(jnp.int32, (LANES, LANES), 0)
    lj = lax.broadcasted_iota(jnp.int32, (LANES, LANES), 1)
    lower = jnp.where(li < lj, 1.0, 0.0).astype(BF16)
    run_off = _dot(jnp.broadcast_to(n_tile, (8, LANES)).astype(BF16), lower)[0:1, :]
    within = earlier + run_off
    out = jnp.zeros((tr, LANES), F32)
    for k in range(TOP_K):
        pos = jnp.sum(jnp.where(hots[k], within, 0.0), axis=-1, keepdims=True)
        out = jnp.where(lane == k, top_i[k], out)
        out = jnp.where(lane == TOP_K + k, pos, out)
        out = jnp.where(lane == 2 * TOP_K + k, ex[k] / den, out)
    o_ref[...] = out
    tcnt_ref[0] = n_tile
    tbef_ref[0] = carry_ref[...]
    total = carry_ref[...] + n_tile
    carry_ref[...] = total
    cnt_ref[...] = total


def _route(lg):
    T = lg.shape[0]
    tr = MOE_TILE
    per_tile = pl.BlockSpec((1, 1, LANES), lambda i: (i, 0, 0))
    return pl.pallas_call(
        _route_kernel,
        grid=(T // tr,),
        in_specs=[pl.BlockSpec((tr, LANES), lambda i: (i, 0))],
        out_specs=[pl.BlockSpec((tr, LANES), lambda i: (i, 0)), per_tile, per_tile,
                   pl.BlockSpec((1, LANES), lambda i: (0, 0))],
        out_shape=[jax.ShapeDtypeStruct((T, LANES), F32), jax.ShapeDtypeStruct((T // tr, 1, LANES), F32),
                   jax.ShapeDtypeStruct((T // tr, 1, LANES), F32), jax.ShapeDtypeStruct((1, LANES), F32)],
        scratch_shapes=[pltpu.VMEM((1, LANES), F32)],
        compiler_params=_cp(("arbitrary",)),
        name="route_topk",
    )(lg)


_ALIGN_BITS = RUN_ALIGN.bit_length() - 1
_TILE_ROWS = MOE_TILE * TOP_K + N_EXPERTS * RUN_ALIGN
_CHUNK_BITS = tuple(range(_ALIGN_BITS, MOE_TILE.bit_length()))
_N_CHUNK = len(_CHUNK_BITS)


def _run_copies(tile, slot, run_dst_ref, run_len_ref, run_off_ref, tile_buf, hbm_ref, sems, to_hbm):
    def one_expert(e, c):
        idx = tile * N_EXPERTS + e
        n, dst, off = run_len_ref[idx], run_dst_ref[idx], run_off_ref[idx]
        for s, b in enumerate(_CHUNK_BITS):
            size = 1 << b

            @pl.when((n & size) != 0)
            def _():
                done = (n >> (b + 1)) << (b + 1)
                loc = tile_buf.at[slot, pl.ds(pl.multiple_of(off + done, RUN_ALIGN), size)]
                far = hbm_ref.at[pl.ds(pl.multiple_of(dst + done, RUN_ALIGN), size)]
                src, dstn = (loc, far) if to_hbm else (far, loc)
                pltpu.make_async_copy(src, dstn, sems.at[slot, s]).start(priority=s % 2)
        return c

    lax.fori_loop(0, N_EXPERTS, one_expert, 0)


def _run_waits(tile, slot, nchunk_ref, tile_buf, hbm_ref, sems, to_hbm):
    for s, b in enumerate(_CHUNK_BITS):
        size = 1 << b
        loc = tile_buf.at[slot, pl.ds(0, size)]
        far = hbm_ref.at[pl.ds(0, size)]
        src, dstn = (loc, far) if to_hbm else (far, loc)
        cp = pltpu.make_async_copy(src, dstn, sems.at[slot, s])
        lax.fori_loop(0, nchunk_ref[tile * _N_CHUNK + s], lambda n, c: (cp.wait(), c)[1], 0)


def _dispatch_kernel(tail_ref, nused_ref, run_dst_ref, run_len_ref, run_off_ref, nchunk_ref,
                     rt_ref, x_ref, o_ref, xs_ref, zero_ref, sems, fill_sem, trail_sem):
    i = pl.program_id(0)
    slot = i % 2
    n_blocks = o_ref.shape[0] // MOE_BLK

    def fill(row, sem):
        return pltpu.make_async_copy(zero_ref, o_ref.at[pl.ds(pl.multiple_of(row, MOE_BLK), MOE_BLK)], sem)

    def trailing(op):
        lax.fori_loop(nused_ref[0], n_blocks, lambda n, c: (op(fill(n * MOE_BLK, trail_sem)), c)[1], 0)

    @pl.when(i == 0)
    def _():
        zero_ref[...] = jnp.zeros_like(zero_ref)
        lax.fori_loop(0, N_EXPERTS, lambda n, c: (fill(tail_ref[n], fill_sem).start(), c)[1], 0)
        trailing(lambda cp: cp.start())
        lax.fori_loop(0, N_EXPERTS, lambda n, c: (fill(tail_ref[n], fill_sem).wait(), c)[1], 0)

    pos_t = rt_ref[...].T
    xb = x_ref[...]
    for r0 in range(0, _TILE_ROWS, MOE_TILE):
        rows = (lax.broadcasted_iota(jnp.int32, (MOE_TILE, MOE_TILE), 0) + r0).astype(F32)
        place = jnp.zeros((MOE_TILE, MOE_TILE), F32)
        for k in range(TOP_K):
            place = jnp.where(rows == pos_t[TOP_K + k:TOP_K + k + 1, :], 1.0, place)
        xs_ref[slot, r0:r0 + MOE_TILE, :] = _dot(place.astype(BF16), xb).astype(BF16)

    _run_copies(i, slot, run_dst_ref, run_len_ref, run_off_ref, xs_ref, o_ref, sems, True)

    @pl.when(i > 0)
    def _():
        _run_waits(i - 1, 1 - slot, nchunk_ref, xs_ref, o_ref, sems, True)

    @pl.when(i == pl.num_programs(0) - 1)
    def _():
        _run_waits(i, slot, nchunk_ref, xs_ref, o_ref, sems, True)
        trailing(lambda cp: cp.wait())


def _dispatch(tail_rows, nused, run_dst, run_len, run_off, nchunk, rt, xn, n_rows):
    T = xn.shape[0]
    tb = MOE_TILE
    return pl.pallas_call(
        _dispatch_kernel,
        grid_spec=pltpu.PrefetchScalarGridSpec(
            num_scalar_prefetch=6,
            grid=(T // tb,),
            in_specs=[pl.BlockSpec((tb, LANES), lambda i, *_: (i, 0)),
                      pl.BlockSpec((tb, D_MODEL), lambda i, *_: (i, 0))],
            out_specs=pl.BlockSpec(memory_space=pl.ANY),
            scratch_shapes=[pltpu.VMEM((2, _TILE_ROWS, D_MODEL), BF16), pltpu.VMEM((MOE_BLK, D_MODEL), BF16),
                            pltpu.SemaphoreType.DMA((2, _N_CHUNK)), pltpu.SemaphoreType.DMA(()),
                            pltpu.SemaphoreType.DMA(())],
        ),
        out_shape=jax.ShapeDtypeStruct((n_rows, D_MODEL), BF16),
        compiler_params=_cp(("arbitrary",)),
        name="moe_dispatch",
    )(tail_rows, nused, run_dst, run_len, run_off, nchunk, rt, xn)


def _expert_kernel(be_ref, nused_ref, first_ref, next_ref, wslot_ref, x_ref, wg_hbm, bg_ref, wu_hbm, bu_ref,
                   wd_hbm, bd_ref, y_ref, stage_ref, wb_ref, sems):
    i = pl.program_id(0)

    def fetch(e, slot):
        return [pltpu.make_async_copy(w.at[e], stage_ref.at[slot, m], sems.at[slot])
                for m, w in enumerate((wg_hbm, wu_hbm, wd_hbm))]

    @pl.when(first_ref[i] != 0)
    def _():
        slot = wslot_ref[i]

        @pl.when(i == 0)
        def _():
            for cp in fetch(be_ref[0], slot):
                cp.start()

        for cp in fetch(be_ref[i], slot):
            cp.wait()
        for m in range(3):
            wb_ref[m] = stage_ref[slot, m].astype(BF16)

        @pl.when(next_ref[i] >= 0)
        def _():
            for cp in fetch(next_ref[i], 1 - slot):
                cp.start()

    @pl.when(i < nused_ref[0])
    def _():
        xb = x_ref[...]
        g = _dot(xb, wb_ref[0]) + bg_ref[0]
        u = _dot(xb, wb_ref[1]) + bu_ref[0]
        g = jnp.minimum(g, SWIGLU_LIMIT)
        u = jnp.clip(u, -SWIGLU_LIMIT, SWIGLU_LIMIT)
        glu = g * _sigmoid(SWIGLU_ALPHA * g)
        y_ref[...] = (_dot(((u + 1.0) * glu).astype(BF16), wb_ref[2]) + bd_ref[0]).astype(BF16)

    @pl.when(i >= nused_ref[0])
    def _():
        y_ref[...] = jnp.zeros_like(y_ref)


def _experts(block_expert, nused, first, next_expert, wslot, x_pad, wg, bg, wu, bu, wd, bd):
    n_rows = x_pad.shape[0]
    wspec = pl.BlockSpec(memory_space=pl.ANY)
    bspec = pl.BlockSpec((1, 1, D_MODEL), lambda i, be, *_: (be[i], 0, 0))
    yspec = pl.BlockSpec((MOE_BLK, D_MODEL), lambda i, *_: (i, 0))
    xspec = pl.BlockSpec((MOE_BLK, D_MODEL), lambda i, be, nu, *_: (jnp.minimum(i, nu[0] - 1), 0))
    return pl.pallas_call(
        _expert_kernel,
        grid_spec=pltpu.PrefetchScalarGridSpec(
            num_scalar_prefetch=5,
            grid=(n_rows // MOE_BLK,),
            in_specs=[xspec, wspec, bspec, wspec, bspec, wspec, bspec],
            out_specs=yspec,
            scratch_shapes=[pltpu.VMEM((2, 3, D_MODEL, D_MODEL), F32), pltpu.VMEM((3, D_MODEL, D_MODEL), BF16),
                            pltpu.SemaphoreType.DMA((2,))],
        ),
        out_shape=jax.ShapeDtypeStruct((n_rows, D_MODEL), BF16),
        compiler_params=_cp(("arbitrary",)),
        name="moe_experts",
    )(block_expert, nused, first, next_expert, wslot, x_pad, wg, bg, wu, bu, wd, bd)


def _tail_kernel(run_dst_ref, run_len_ref, run_off_ref, nchunk_ref, y_ref, rt_ref, h_ref, p_ref, npl_ref,
                 wpg_ref, wpp_ref, nfin_ref, o_ref, ys_ref, sems):
    i = pl.program_id(0)
    slot = i % 2
    fetch = functools.partial(_run_copies, run_dst_ref=run_dst_ref, run_len_ref=run_len_ref,
                              run_off_ref=run_off_ref, tile_buf=ys_ref, hbm_ref=y_ref, sems=sems, to_hbm=False)

    @pl.when(i == 0)
    def _():
        ys_ref[...] = jnp.zeros_like(ys_ref)
        fetch(i, slot)

    @pl.when(i + 1 < pl.num_programs(0))
    def _():
        fetch(i + 1, 1 - slot)

    _run_waits(i, slot, nchunk_ref, ys_ref, y_ref, sems, False)

    rt = rt_ref[...]
    cols = lax.broadcasted_iota(jnp.int32, (MOE_TILE, _TILE_ROWS), 1).astype(F32)
    weight = jnp.zeros((MOE_TILE, _TILE_ROWS), F32)
    for k in range(TOP_K):
        weight = jnp.where(cols == rt[:, TOP_K + k:TOP_K + k + 1],
                           rt[:, 2 * TOP_K + k:2 * TOP_K + k + 1], weight)
    h = h_ref[...] + _dot(weight.astype(BF16), ys_ref[slot])
    pg = _sigmoid(_dot(_rms(h, npl_ref[...]).astype(BF16), wpg_ref[...]))
    h = h + pg * _dot(p_ref[...].astype(BF16), wpp_ref[...])
    o_ref[...] = _rms(h, nfin_ref[...])


def _tail(run_dst, run_len, run_off, nchunk, y_pad, rt, h1, p2, npl, wpg, wpp, nfin):
    T = h1.shape[0]
    tc = MOE_TILE
    row = lambda n: pl.BlockSpec((tc, n), lambda i, *_: (i, 0))
    full = lambda a: pl.BlockSpec(a.shape, lambda i, *_: (0,) * a.ndim)
    return pl.pallas_call(
        _tail_kernel,
        grid_spec=pltpu.PrefetchScalarGridSpec(
            num_scalar_prefetch=4,
            grid=(T // tc,),
            in_specs=[pl.BlockSpec(memory_space=pl.ANY), row(LANES), row(D_MODEL), row(PLE_DIM),
                      full(npl), full(wpg), full(wpp), full(nfin)],
            out_specs=row(D_MODEL),
            scratch_shapes=[pltpu.VMEM((2, _TILE_ROWS, D_MODEL), BF16), pltpu.SemaphoreType.DMA((2, _N_CHUNK))],
        ),
        out_shape=jax.ShapeDtypeStruct((T, D_MODEL), F32),
        compiler_params=_cp(("arbitrary",)),
        name="moe_combine_tail",
    )(run_dst, run_len, run_off, nchunk, y_pad, rt, h1, p2, npl, wpg, wpp, nfin)


def _layer(h, p_l, norm_mix, w_in, b_forget, w_attn_out, conv_w, conv_b, conv_ln_g, conv_ln_b, w_conv_out, w_o,
           norm_ffn, w_router, b_router, w_gate, b_gate, w_up, b_up, w_down, b_down, norm_ple, w_ple_gate,
           w_ple_proj):
    B, S, D = h.shape
    T = B * S
    row2 = lambda a: a.reshape(1, -1)

    o_f = 3 * ATTN_WIDTH
    o_conv = o_f + N_HEADS
    reps = LANES // N_HEADS
    w_f = jnp.tile(w_in[:, o_f:o_conv], (1, reps))
    w_all = jnp.concatenate([w_in[:, :ATTN_WIDTH] * (HEAD_DIM ** -0.5), w_in[:, ATTN_WIDTH:o_f],
                             w_in[:, o_conv:], w_f], axis=1).astype(BF16)
    bf = jnp.tile(b_forget, reps).reshape(1, LANES)

    qa, ka, va, z, sga, sgc = _inproj(h, row2(norm_mix), w_all, bf, *_bias_routing())
    attn = _attention(qa, ka, va)

    cw = jnp.concatenate([conv_w, jnp.zeros((CONV_HALO - CONV_K, CONV_WIDTH), F32)], axis=0)
    wr = jnp.concatenate([w_router, jnp.zeros((D, LANES - N_EXPERTS), F32)], axis=1).astype(BF16)
    br = jnp.concatenate([b_router, jnp.full((LANES - N_EXPERTS,), NEG_INF, F32)]).reshape(1, LANES)
    h1, xn2, logits = _mix(
        z, cw, row2(conv_b), row2(conv_ln_g), row2(conv_ln_b), attn,
        sga, sgc, h, w_attn_out.astype(BF16), w_conv_out.astype(BF16),
        w_o.astype(BF16), row2(norm_ffn), wr, br)

    rt, tile_cnt, tile_before, counts = _route(logits.reshape(T, LANES))
    cnt = counts[0, :N_EXPERTS].astype(jnp.int32)
    padded = ((cnt + MOE_BLK - 1) // MOE_BLK) * MOE_BLK
    pad_end = jnp.cumsum(padded)
    pad_start = pad_end - padded
    max_rows = T * TOP_K + (T // MOE_TILE) * N_EXPERTS * (RUN_ALIGN - 1)
    n_blocks = -(-max_rows // MOE_BLK) + N_EXPERTS
    n_rows = n_blocks * MOE_BLK
    n_used = pad_end[-1] // MOE_BLK
    run_len = tile_cnt[:, 0, :N_EXPERTS].astype(jnp.int32)
    run_off = jnp.cumsum(run_len, axis=1) - run_len
    run_dst = pad_start[None, :] + tile_before[:, 0, :N_EXPERTS].astype(jnp.int32)
    bits = jnp.asarray(_CHUNK_BITS, jnp.int32)
    nchunk = jnp.sum((run_len[:, :, None] >> bits) & 1, axis=1).astype(jnp.int32)
    runs = (run_dst.reshape(-1), run_len.reshape(-1), run_off.reshape(-1), nchunk.reshape(-1))
    blk = jnp.arange(n_blocks, dtype=jnp.int32)
    block_expert = jnp.minimum(
        jnp.sum((pad_end[None, :] <= (blk * MOE_BLK)[:, None]).astype(jnp.int32), axis=1), N_EXPERTS - 1)
    fill_rows = jnp.where(cnt > 0, pad_end - MOE_BLK, n_used * MOE_BLK).astype(jnp.int32)

    nused = n_used.astype(jnp.int32).reshape(1)
    first = (blk < n_used) & ((blk == 0) | (block_expert != jnp.roll(block_expert, 1)))
    wslot = (jnp.cumsum(first.astype(jnp.int32)) - 1) % 2
    later_first = jnp.where(first[None, :] & (blk[None, :] > blk[:, None]), blk[None, :], n_blocks)
    next_pos = jnp.min(later_first, axis=1)
    next_expert = jnp.sum(jnp.where(blk[None, :] == next_pos[:, None], block_expert[None, :], 0), axis=1)
    next_expert = jnp.where(next_pos < n_blocks, next_expert, -1).astype(jnp.int32)

    x_pad = _dispatch(fill_rows, nused, *runs, rt, xn2.reshape(T, D), n_rows)
    y_pad = _experts(block_expert, nused, first.astype(jnp.int32), next_expert, wslot.astype(jnp.int32), x_pad,
                     w_gate, b_gate.reshape(N_EXPERTS, 1, D), w_up,
                     b_up.reshape(N_EXPERTS, 1, D), w_down, b_down.reshape(N_EXPERTS, 1, D))
    return _tail, (*runs, y_pad, rt, h1.reshape(T, D),
                   p_l.reshape(T, PLE_DIM), row2(norm_ple), w_ple_gate.astype(BF16), w_ple_proj.astype(BF16))


def kernel(x, p, norm_mix, w_in, b_forget, w_attn_out, conv_w, conv_b, conv_ln_g, conv_ln_b, w_conv_out, w_o,
           norm_ffn, w_router, b_router, w_gate, b_gate, w_up, b_up, w_down, b_down, norm_ple, w_ple_gate,
           w_ple_proj, norm_final):
    B, S, D = x.shape
    assert p.shape[0] == 1 and D == D_MODEL
    tail, args = _layer(x, p[0], norm_mix[0], w_in[0], b_forget[0], w_attn_out[0], conv_w[0], conv_b[0],
                        conv_ln_g[0], conv_ln_b[0], w_conv_out[0], w_o[0], norm_ffn[0], w_router[0],
                        b_router[0], w_gate[0], b_gate[0], w_up[0], b_up[0], w_down[0], b_down[0],
                        norm_ple[0], w_ple_gate[0], w_ple_proj[0])
    return tail(*args, norm_final.reshape(1, D)).reshape(B, S, D)
```

```python
import functools

import jax
import jax.numpy as jnp
import numpy as np
from jax import lax
from jax.experimental import pallas as pl
from jax.experimental.pallas import tpu as pltpu

F32 = jnp.float32
BF16 = jnp.bfloat16

D_MODEL = 1024
HEAD_DIM = 64
N_HEADS = 8
ATTN_WIDTH = N_HEADS * HEAD_DIM
CONV_WIDTH = 512
CONV_K = 31
N_EXPERTS = 32
TOP_K = 4
SWIGLU_LIMIT = 7.0
SWIGLU_ALPHA = 1.702
PLE_DIM = 256
RMS_EPS = 1e-6
LN_EPS = 1e-5
NEG_INF = -1e30

LANES = 128
SUBLANES = 8
VMEM_LIMIT = 56 * 1024 * 1024

ROW_TILE = 512
ATTN_T = 512
HEAD_PAD = 128
CONV_HALO = 32
MOE_BLK = 512
MOE_TILE = 512
RUN_ALIGN = 16


def _cp(sem):
    return pltpu.CompilerParams(dimension_semantics=sem, vmem_limit_bytes=VMEM_LIMIT)


def _dot(a, b):
    return jnp.dot(a, b, preferred_element_type=F32)


def _sigmoid(x):
    return 1.0 / (1.0 + jnp.exp(-x))


def _rms(x, g):
    return x * lax.rsqrt(jnp.mean(x * x, axis=-1, keepdims=True) + RMS_EPS) * g


_C_QKV = 3 * ATTN_WIDTH
_C_CONV = _C_QKV + 2 * CONV_WIDTH
_C_GA = _C_CONV + D_MODEL
_C_GC = _C_GA + D_MODEL
_C_END = _C_GC + LANES


_HP_ALL = N_HEADS * HEAD_PAD
_N_CSPLIT = 3


def _split3(x):
    hi = x.astype(BF16)
    r1 = x - hi.astype(F32)
    mid = r1.astype(BF16)
    lo = (r1 - mid.astype(F32)).astype(BF16)
    return hi, mid, lo


def _store_heads(o_ref, src, aux):
    low = lax.broadcasted_iota(jnp.int32, (src.shape[0], LANES), 1) < HEAD_DIM
    for h in range(N_HEADS):
        chunk = src[:, (h // 2) * LANES:(h // 2 + 1) * LANES]
        if h % 2:
            chunk = pltpu.roll(chunk, HEAD_DIM, 1)
        o_ref[0, :, h * HEAD_PAD:(h + 1) * HEAD_PAD] = jnp.where(
            low, chunk, aux[:, h * HEAD_PAD:(h + 1) * HEAD_PAD]).astype(BF16)


def _conv_act(z, zs_ref, ph_ref, cv_ref, cw_ref, cb_ref, lng_ref, lnb_ref):
    ts = z.shape[0]
    zs_ref[0:CONV_HALO, :] = zs_ref[ts:ts + CONV_HALO, :]
    zs_ref[CONV_HALO:CONV_HALO + ts, :] = z
    base = CONV_HALO - (CONV_K - 1)
    for ph in range(SUBLANES):
        rows = ts + SUBLANES * ((CONV_K - 1 - ph) // SUBLANES)
        ph_ref[ph, 0:rows, :] = zs_ref[base + ph:base + ph + rows, :]
    rc = 128
    for c0 in range(0, CONV_WIDTH, LANES):
        for r0 in range(0, ts, rc):
            acc = jnp.zeros((rc, LANES), F32)
            for j in range(CONV_K):
                a0 = r0 + SUBLANES * (j // SUBLANES)
                acc = acc + cw_ref[j:j + 1, c0:c0 + LANES] * ph_ref[j % SUBLANES, a0:a0 + rc, c0:c0 + LANES]
            cv_ref[r0:r0 + rc, c0:c0 + LANES] = acc
    cv = cv_ref[...] + cb_ref[...]
    mu = jnp.mean(cv, axis=-1, keepdims=True)
    d = cv - mu
    var = jnp.mean(d * d, axis=-1, keepdims=True)
    y = d * lax.rsqrt(var + LN_EPS) * lng_ref[...] + lnb_ref[...]
    return (y * _sigmoid(y)).astype(BF16)


def _inproj_kernel(x_ref, g_ref, w_ref, bf_ref, selq_ref, selk_ref, auxq_ref, auxk_ref, auxv_ref,
                   cw_ref, cb_ref, lng_ref, lnb_ref, qa_ref, ka_ref, va_ref, act_ref, sga_ref, sgc_ref,
                   carry_ref, zs_ref, ph_ref, cv_ref):
    tm = x_ref.shape[1]

    @pl.when(pl.program_id(1) == 0)
    def _():
        carry_ref[...] = jnp.zeros_like(carry_ref)
        zs_ref[tm:tm + CONV_HALO, :] = jnp.zeros((CONV_HALO, CONV_WIDTH), F32)

    xn = _rms(x_ref[0], g_ref[...]).astype(BF16)

    f = _dot(xn, w_ref[:, _C_GC:_C_END]) + bf_ref[...]
    lf = jnp.minimum(f, 0.0) - jnp.log(1.0 + jnp.exp(-jnp.abs(f)))
    r = lax.broadcasted_iota(jnp.int32, (tm, tm), 0)
    c = lax.broadcasted_iota(jnp.int32, (tm, tm), 1)
    tri = jnp.where(c <= r, 1.0, 0.0).astype(BF16)
    hi, mid, lo = _split3(lf)
    cs = _dot(tri, hi) + _dot(tri, mid) + _dot(tri, lo) + carry_ref[...]
    carry_ref[...] = cs[tm - 1:tm, :]
    chi, cmid, clo = (a.astype(F32) for a in _split3(cs))
    group = lax.broadcasted_iota(jnp.int32, (tm, LANES), 1) // N_HEADS
    csel = jnp.where(group == 0, chi, jnp.where(group == 1, cmid, jnp.where(
        group == 2, clo, jnp.where(group == 3, -chi, jnp.where(group == 4, -cmid, -clo))))).astype(BF16)
    aux_q = _dot(csel, selq_ref[...]) + auxq_ref[...]
    aux_k = _dot(csel, selk_ref[...]) + auxk_ref[...]

    qkv = _dot(xn, w_ref[:, 0:_C_QKV])
    _store_heads(qa_ref, qkv[:, 0:ATTN_WIDTH], aux_q)
    _store_heads(ka_ref, qkv[:, ATTN_WIDTH:2 * ATTN_WIDTH], aux_k)
    _store_heads(va_ref, qkv[:, 2 * ATTN_WIDTH:3 * ATTN_WIDTH], auxv_ref[...])
    u = _dot(xn, w_ref[:, _C_QKV:_C_CONV])
    z = u[:, 0:CONV_WIDTH] * _sigmoid(u[:, CONV_WIDTH:])
    act_ref[0] = _conv_act(z, zs_ref, ph_ref, cv_ref, cw_ref, cb_ref, lng_ref, lnb_ref)
    sga_ref[0] = _sigmoid(_dot(xn, w_ref[:, _C_CONV:_C_GA])).astype(BF16)
    sgc_ref[0] = _sigmoid(_dot(xn, w_ref[:, _C_GA:_C_GC])).astype(BF16)


def _inproj(x, g, w, bf, selq, selk, auxq, auxk, auxv, cw, cb, lng, lnb):
    B, S, _ = x.shape
    tm = ROW_TILE
    row = lambda n: pl.BlockSpec((1, tm, n), lambda b, i: (b, i, 0))
    full = lambda a: pl.BlockSpec(a.shape, lambda b, i: (0,) * a.ndim, pipeline_mode=pl.Buffered(1))
    sds = lambda n, dt: jax.ShapeDtypeStruct((B, S, n), dt)
    consts = (g, w, bf, selq, selk, auxq, auxk, auxv, cw, cb, lng, lnb)
    return pl.pallas_call(
        _inproj_kernel,
        grid=(B, S // tm),
        in_specs=[row(D_MODEL)] + [full(a) for a in consts],
        out_specs=[row(_HP_ALL), row(_HP_ALL), row(_HP_ALL), row(CONV_WIDTH), row(D_MODEL), row(D_MODEL)],
        out_shape=[sds(_HP_ALL, BF16)] * 3 + [sds(CONV_WIDTH, BF16), sds(D_MODEL, BF16), sds(D_MODEL, BF16)],
        scratch_shapes=[pltpu.VMEM((1, LANES), F32), pltpu.VMEM((CONV_HALO + tm, CONV_WIDTH), F32),
                        pltpu.VMEM((SUBLANES, tm + CONV_HALO - SUBLANES, CONV_WIDTH), F32),
                        pltpu.VMEM((tm, CONV_WIDTH), F32)],
        compiler_params=_cp(("parallel", "arbitrary")),
        name="inproj",
    )(x, g, w, bf, selq, selk, auxq, auxk, auxv, cw, cb, lng, lnb)


def _bias_routing():
    selq = np.zeros((LANES, _HP_ALL), np.float32)
    selk = np.zeros((LANES, _HP_ALL), np.float32)
    auxq = np.zeros((1, _HP_ALL), np.float32)
    auxk = np.zeros((1, _HP_ALL), np.float32)
    auxv = np.zeros((1, _HP_ALL), np.float32)
    for h in range(N_HEADS):
        base = h * HEAD_PAD + HEAD_DIM
        for g in range(_N_CSPLIT):
            selq[g * N_HEADS + h, base + g] = 1.0
            auxk[0, base + g] = 1.0
            auxq[0, base + _N_CSPLIT + g] = 1.0
            selk[(_N_CSPLIT + g) * N_HEADS + h, base + _N_CSPLIT + g] = 1.0
        auxv[0, base] = 1.0
    return (jnp.asarray(selq, BF16), jnp.asarray(selk, BF16), jnp.asarray(auxq), jnp.asarray(auxk),
            jnp.asarray(auxv))


_HEADS_PER_STEP = 4


def _attn_kernel(qa_ref, ka_ref, va_ref, o_ref, s_ref):
    i = pl.program_id(2)
    t = ATTN_T
    dn = (((1,), (1,)), ((), ()))
    nc = t // LANES

    def scores(j, slot):
        r0 = pl.multiple_of(j * t, t)
        for hh in range(_HEADS_PER_STEP):
            lanes = slice(hh * HEAD_PAD, (hh + 1) * HEAD_PAD)
            s_ref[slot, hh] = lax.dot_general(qa_ref[0, :, lanes], ka_ref[0, pl.ds(r0, t), lanes], dn,
                                              preferred_element_type=F32)

    def consume(j, slot, carry, masked):
        r0 = pl.multiple_of(j * t, t)
        new = []
        for hh in range(_HEADS_PER_STEP):
            m, acc = carry[hh]
            lanes = slice(hh * HEAD_PAD, (hh + 1) * HEAD_PAD)
            s = s_ref[slot, hh]
            if masked:
                row = lax.broadcasted_iota(jnp.int32, (t, t), 0)
                col = lax.broadcasted_iota(jnp.int32, (t, t), 1)
                s = jnp.where(col <= row, s, NEG_INF)
            sc = [s[:, c * LANES:(c + 1) * LANES] for c in range(nc)]
            mloc = functools.reduce(jnp.maximum, sc)
            m_new = jnp.maximum(m, jnp.max(mloc, axis=-1, keepdims=True))
            p = jnp.concatenate([jnp.exp(x - m_new).astype(BF16) for x in sc], axis=1)
            acc = jnp.exp(m - m_new) * acc + _dot(p, va_ref[0, pl.ds(r0, t), lanes])
            new.append((m_new, acc))
        return tuple(new)

    def finish(carry):
        outs = [acc[:, 0:HEAD_DIM] / acc[:, HEAD_DIM:HEAD_DIM + 1] for _, acc in carry]
        o_ref[0] = jnp.concatenate(outs, axis=-1).astype(BF16)

    def pair(pp, carry):
        j = 2 * pp
        scores(j + 1, 1)
        carry = consume(j, 0, carry, False)
        scores(j + 2, 0)
        return consume(j + 1, 1, carry, False)

    init = tuple((jnp.full((t, LANES), NEG_INF, F32), jnp.zeros((t, HEAD_PAD), F32))
                 for _ in range(_HEADS_PER_STEP))
    scores(0, 0)
    carry = lax.fori_loop(0, i // 2, pair, init)

    @pl.when(i % 2 == 0)
    def _():
        finish(consume(i, 0, carry, True))

    @pl.when(i % 2 == 1)
    def _():
        scores(i, 1)
        finish(consume(i, 1, consume(i - 1, 0, carry, False), True))


def _attention(qa, ka, va):
    B, S, _ = qa.shape
    t = ATTN_T
    w = _HEADS_PER_STEP * HEAD_PAD
    return pl.pallas_call(
        _attn_kernel,
        grid=(B, N_HEADS // _HEADS_PER_STEP, S // t),
        in_specs=[pl.BlockSpec((1, t, w), lambda b, h, i: (b, i, h)),
                  pl.BlockSpec((1, S, w), lambda b, h, i: (b, 0, h)),
                  pl.BlockSpec((1, S, w), lambda b, h, i: (b, 0, h))],
        out_specs=pl.BlockSpec((1, t, _HEADS_PER_STEP * HEAD_DIM), lambda b, h, i: (b, i, h)),
        out_shape=jax.ShapeDtypeStruct((B, S, ATTN_WIDTH), BF16),
        scratch_shapes=[pltpu.VMEM((2, _HEADS_PER_STEP, t, t), F32)],
        compiler_params=_cp(("parallel", "parallel", "arbitrary")),
        name="fox_attention",
    )(qa, ka, va)


def _mix_kernel(act_ref, attn_ref, sga_ref, sgc_ref, x_ref, wa_ref, wc_ref, wo_ref, nf_ref, wr_ref, br_ref,
                h_ref, xn_ref, rt_ref, tcnt_ref, tbef_ref, cnt_ref, carry_ref):
    i = pl.program_id(1)
    branch_c = _dot(act_ref[0], wc_ref[...])
    branch_a = _dot(attn_ref[0], wa_ref[...])
    merged = sga_ref[0].astype(F32) * branch_a + sgc_ref[0].astype(F32) * branch_c
    h = x_ref[0] + _dot(merged.astype(BF16), wo_ref[...])
    h_ref[0] = h
    xn = _rms(h, nf_ref[...]).astype(BF16)
    xn_ref[0] = xn

    @pl.when((pl.program_id(0) == 0) & (i == 0))
    def _():
        carry_ref[...] = jnp.zeros_like(carry_ref)

    rt, n_tile = _route_tile(_dot(xn, wr_ref[...]) + br_ref[...])
    rt_ref[0] = rt
    tcnt_ref[0] = n_tile
    tbef_ref[0] = carry_ref[...]
    total = carry_ref[...] + n_tile
    carry_ref[...] = total
    cnt_ref[...] = total


def _mix(act, attn, sga, sgc, x, wa, wc, wo, nf, wr, br):
    B, S, _ = x.shape
    ts = ROW_TILE
    assert ts == MOE_TILE, "routing runs on the mix kernel's row tiles"
    nt = S // ts
    row = lambda n: pl.BlockSpec((1, ts, n), lambda b, i: (b, i, 0))
    full = lambda a: pl.BlockSpec(a.shape, lambda b, i: (0,) * a.ndim)
    per_tile = pl.BlockSpec((1, 1, LANES), lambda b, i: (b * nt + i, 0, 0))
    return pl.pallas_call(
        _mix_kernel,
        grid=(B, nt),
        in_specs=[row(CONV_WIDTH), row(ATTN_WIDTH), row(D_MODEL), row(D_MODEL), row(D_MODEL),
                  full(wa), full(wc), full(wo), full(nf), full(wr), full(br)],
        out_specs=[row(D_MODEL), row(D_MODEL), row(LANES), per_tile, per_tile,
                   pl.BlockSpec((1, LANES), lambda b, i: (0, 0))],
        out_shape=[jax.ShapeDtypeStruct((B, S, D_MODEL), F32), jax.ShapeDtypeStruct((B, S, D_MODEL), BF16),
                   jax.ShapeDtypeStruct((B, S, LANES), F32), jax.ShapeDtypeStruct((B * nt, 1, LANES), F32),
                   jax.ShapeDtypeStruct((B * nt, 1, LANES), F32), jax.ShapeDtypeStruct((1, LANES), F32)],
        scratch_shapes=[pltpu.VMEM((1, LANES), F32)],
        compiler_params=_cp(("arbitrary", "arbitrary")),
        name="mix_merge",
    )(act, attn, sga, sgc, x, wa, wc, wo, nf, wr, br)


def _route_tile(vals):
    tr = vals.shape[0]
    lane = lax.broadcasted_iota(jnp.int32, (tr, LANES), 1).astype(F32)
    top_v, top_i, hots = [], [], []
    for _ in range(TOP_K):
        m = jnp.max(vals, axis=-1, keepdims=True)
        idx = jnp.min(jnp.where(vals == m, lane, float(LANES)), axis=-1, keepdims=True)
        hot = lane == idx
        vals = jnp.where(hot, -jnp.inf, vals)
        top_v.append(m)
        top_i.append(idx)
        hots.append(hot)
    ex = [jnp.exp(v - top_v[0]) for v in top_v]
    den = ex[0] + ex[1] + ex[2] + ex[3]
    onehot = jnp.zeros((tr, LANES), F32)
    for hot in hots:
        onehot = onehot + jnp.where(hot, 1.0, 0.0)
    r = lax.broadcasted_iota(jnp.int32, (tr, tr), 0)
    c = lax.broadcasted_iota(jnp.int32, (tr, tr), 1)
    strict = jnp.where(c < r, 1.0, 0.0).astype(BF16)
    earlier = _dot(strict, onehot.astype(BF16))
    n_tile = jnp.sum(onehot, axis=0, keepdims=True)
    n_tile = jnp.floor((n_tile + (RUN_ALIGN - 1)) * (1.0 / RUN_ALIGN)) * RUN_ALIGN
    li = lax.broadcasted_iota(jnp.int32, (LANES, LANES), 0)
    lj = lax.broadcasted_iota(jnp.int32, (LANES, LANES), 1)
    lower = jnp.where(li < lj, 1.0, 0.0).astype(BF16)
    run_off = _dot(jnp.broadcast_to(n_tile, (8, LANES)).astype(BF16), lower)[0:1, :]
    within = earlier + run_off
    out = jnp.zeros((tr, LANES), F32)
    for k in range(TOP_K):
        pos = jnp.sum(jnp.where(hots[k], within, 0.0), axis=-1, keepdims=True)
        out = jnp.where(lane == k, top_i[k], out)
        out = jnp.where(lane == TOP_K + k, pos, out)
        out = jnp.where(lane == 2 * TOP_K + k, ex[k] / den, out)
    return out, n_tile


_ALIGN_BITS = RUN_ALIGN.bit_length() - 1
_TILE_ROWS = MOE_TILE * TOP_K + N_EXPERTS * RUN_ALIGN
_CHUNK_BITS = tuple(range(_ALIGN_BITS, MOE_TILE.bit_length()))
_N_CHUNK = len(_CHUNK_BITS)


def _run_copies(tile, slot, run_dst_ref, run_len_ref, run_off_ref, tile_buf, hbm_ref, sems, to_hbm):
    def one_expert(e, c):
        idx = tile * N_EXPERTS + e
        n, dst, off = run_len_ref[idx], run_dst_ref[idx], run_off_ref[idx]
        for s, b in enumerate(_CHUNK_BITS):
            size = 1 << b

            @pl.when((n & size) != 0)
            def _():
                done = (n >> (b + 1)) << (b + 1)
                loc = tile_buf.at[slot, pl.ds(pl.multiple_of(off + done, RUN_ALIGN), size)]
                far = hbm_ref.at[pl.ds(pl.multiple_of(dst + done, RUN_ALIGN), size)]
                src, dstn = (loc, far) if to_hbm else (far, loc)
                pltpu.make_async_copy(src, dstn, sems.at[slot, s]).start(priority=s % 2)
        return c

    lax.fori_loop(0, N_EXPERTS, one_expert, 0)


def _run_waits(tile, slot, nchunk_ref, tile_buf, hbm_ref, sems, to_hbm):
    for s, b in enumerate(_CHUNK_BITS):
        size = 1 << b
        loc = tile_buf.at[slot, pl.ds(0, size)]
        far = hbm_ref.at[pl.ds(0, size)]
        src, dstn = (loc, far) if to_hbm else (far, loc)
        cp = pltpu.make_async_copy(src, dstn, sems.at[slot, s])
        lax.fori_loop(0, nchunk_ref[tile * _N_CHUNK + s], lambda n, c: (cp.wait(), c)[1], 0)


def _dispatch_kernel(tail_ref, nused_ref, run_dst_ref, run_len_ref, run_off_ref, nchunk_ref,
                     rt_ref, x_ref, o_ref, xs_ref, zero_ref, sems, fill_sem, trail_sem):
    i = pl.program_id(0)
    slot = i % 2
    n_blocks = o_ref.shape[0] // MOE_BLK

    def fill(row, sem):
        return pltpu.make_async_copy(zero_ref, o_ref.at[pl.ds(pl.multiple_of(row, MOE_BLK), MOE_BLK)], sem)

    def trailing(op):
        lax.fori_loop(nused_ref[0], n_blocks, lambda n, c: (op(fill(n * MOE_BLK, trail_sem)), c)[1], 0)

    @pl.when(i == 0)
    def _():
        zero_ref[...] = jnp.zeros_like(zero_ref)
        lax.fori_loop(0, N_EXPERTS, lambda n, c: (fill(tail_ref[n], fill_sem).start(), c)[1], 0)
        trailing(lambda cp: cp.start())
        lax.fori_loop(0, N_EXPERTS, lambda n, c: (fill(tail_ref[n], fill_sem).wait(), c)[1], 0)

    pos_t = rt_ref[...].T
    xb = x_ref[...]
    for r0 in range(0, _TILE_ROWS, MOE_TILE):
        rows = (lax.broadcasted_iota(jnp.int32, (MOE_TILE, MOE_TILE), 0) + r0).astype(F32)
        place = jnp.zeros((MOE_TILE, MOE_TILE), F32)
        for k in range(TOP_K):
            place = jnp.where(rows == pos_t[TOP_K + k:TOP_K + k + 1, :], 1.0, place)
        xs_ref[slot, r0:r0 + MOE_TILE, :] = _dot(place.astype(BF16), xb).astype(BF16)

    _run_copies(i, slot, run_dst_ref, run_len_ref, run_off_ref, xs_ref, o_ref, sems, True)

    @pl.when(i > 0)
    def _():
        _run_waits(i - 1, 1 - slot, nchunk_ref, xs_ref, o_ref, sems, True)

    @pl.when(i == pl.num_programs(0) - 1)
    def _():
        _run_waits(i, slot, nchunk_ref, xs_ref, o_ref, sems, True)
        trailing(lambda cp: cp.wait())


def _dispatch(tail_rows, nused, run_dst, run_len, run_off, nchunk, rt, xn, n_rows):
    T = xn.shape[0]
    tb = MOE_TILE
    return pl.pallas_call(
        _dispatch_kernel,
        grid_spec=pltpu.PrefetchScalarGridSpec(
            num_scalar_prefetch=6,
            grid=(T // tb,),
            in_specs=[pl.BlockSpec((tb, LANES), lambda i, *_: (i, 0)),
                      pl.BlockSpec((tb, D_MODEL), lambda i, *_: (i, 0))],
            out_specs=pl.BlockSpec(memory_space=pl.ANY),
            scratch_shapes=[pltpu.VMEM((2, _TILE_ROWS, D_MODEL), BF16), pltpu.VMEM((MOE_BLK, D_MODEL), BF16),
                            pltpu.SemaphoreType.DMA((2, _N_CHUNK)), pltpu.SemaphoreType.DMA(()),
                            pltpu.SemaphoreType.DMA(())],
        ),
        out_shape=jax.ShapeDtypeStruct((n_rows, D_MODEL), BF16),
        compiler_params=_cp(("arbitrary",)),
        name="moe_dispatch",
    )(tail_rows, nused, run_dst, run_len, run_off, nchunk, rt, xn)


def _expert_kernel(be_ref, nused_ref, first_ref, next_ref, wslot_ref, x_ref, wg_hbm, bg_ref, wu_hbm, bu_ref,
                   wd_hbm, bd_ref, y_ref, stage_ref, wb_ref, sems):
    i = pl.program_id(0)

    def fetch(e, slot):
        return [pltpu.make_async_copy(w.at[e], stage_ref.at[slot, m], sems.at[slot])
                for m, w in enumerate((wg_hbm, wu_hbm, wd_hbm))]

    @pl.when(first_ref[i] != 0)
    def _():
        slot = wslot_ref[i]

        @pl.when(i == 0)
        def _():
            for cp in fetch(be_ref[0], slot):
                cp.start()

        for cp in fetch(be_ref[i], slot):
            cp.wait()
        for m in range(3):
            wb_ref[m] = stage_ref[slot, m].astype(BF16)

        @pl.when(next_ref[i] >= 0)
        def _():
            for cp in fetch(next_ref[i], 1 - slot):
                cp.start()

    @pl.when(i < nused_ref[0])
    def _():
        xb = x_ref[...]
        g = _dot(xb, wb_ref[0]) + bg_ref[0]
        u = _dot(xb, wb_ref[1]) + bu_ref[0]
        g = jnp.minimum(g, SWIGLU_LIMIT)
        u = jnp.clip(u, -SWIGLU_LIMIT, SWIGLU_LIMIT)
        glu = g * _sigmoid(SWIGLU_ALPHA * g)
        y_ref[...] = (_dot(((u + 1.0) * glu).astype(BF16), wb_ref[2]) + bd_ref[0]).astype(BF16)

    @pl.when(i >= nused_ref[0])
    def _():
        y_ref[...] = jnp.zeros_like(y_ref)


def _experts(block_expert, nused, first, next_expert, wslot, x_pad, wg, bg, wu, bu, wd, bd):
    n_rows = x_pad.shape[0]
    wspec = pl.BlockSpec(memory_space=pl.ANY)
    bspec = pl.BlockSpec((1, 1, D_MODEL), lambda i, be, *_: (be[i], 0, 0))
    yspec = pl.BlockSpec((MOE_BLK, D_MODEL), lambda i, *_: (i, 0))
    xspec = pl.BlockSpec((MOE_BLK, D_MODEL), lambda i, be, nu, *_: (jnp.minimum(i, nu[0] - 1), 0))
    return pl.pallas_call(
        _expert_kernel,
        grid_spec=pltpu.PrefetchScalarGridSpec(
            num_scalar_prefetch=5,
            grid=(n_rows // MOE_BLK,),
            in_specs=[xspec, wspec, bspec, wspec, bspec, wspec, bspec],
            out_specs=yspec,
            scratch_shapes=[pltpu.VMEM((2, 3, D_MODEL, D_MODEL), F32), pltpu.VMEM((3, D_MODEL, D_MODEL), BF16),
                            pltpu.SemaphoreType.DMA((2,))],
        ),
        out_shape=jax.ShapeDtypeStruct((n_rows, D_MODEL), BF16),
        compiler_params=_cp(("arbitrary",)),
        name="moe_experts",
    )(block_expert, nused, first, next_expert, wslot, x_pad, wg, bg, wu, bu, wd, bd)


def _tail_kernel(run_dst_ref, run_len_ref, run_off_ref, nchunk_ref, y_ref, rt_ref, h_ref, p_ref, npl_ref,
                 wpg_ref, wpp_ref, nfin_ref, o_ref, ys_ref, sems):
    i = pl.program_id(0)
    slot = i % 2
    fetch = functools.partial(_run_copies, run_dst_ref=run_dst_ref, run_len_ref=run_len_ref,
                              run_off_ref=run_off_ref, tile_buf=ys_ref, hbm_ref=y_ref, sems=sems, to_hbm=False)

    @pl.when(i == 0)
    def _():
        ys_ref[...] = jnp.zeros_like(ys_ref)
        fetch(i, slot)

    @pl.when(i + 1 < pl.num_programs(0))
    def _():
        fetch(i + 1, 1 - slot)

    _run_waits(i, slot, nchunk_ref, ys_ref, y_ref, sems, False)

    rt = rt_ref[...]
    cols = lax.broadcasted_iota(jnp.int32, (MOE_TILE, _TILE_ROWS), 1).astype(F32)
    weight = jnp.zeros((MOE_TILE, _TILE_ROWS), F32)
    for k in range(TOP_K):
        weight = jnp.where(cols == rt[:, TOP_K + k:TOP_K + k + 1],
                           rt[:, 2 * TOP_K + k:2 * TOP_K + k + 1], weight)
    h = h_ref[...] + _dot(weight.astype(BF16), ys_ref[slot])
    pg = _sigmoid(_dot(_rms(h, npl_ref[...]).astype(BF16), wpg_ref[...]))
    h = h + pg * _dot(p_ref[...].astype(BF16), wpp_ref[...])
    o_ref[...] = _rms(h, nfin_ref[...])


def _tail(run_dst, run_len, run_off, nchunk, y_pad, rt, h1, p2, npl, wpg, wpp, nfin):
    T = h1.shape[0]
    tc = MOE_TILE
    row = lambda n: pl.BlockSpec((tc, n), lambda i, *_: (i, 0))
    full = lambda a: pl.BlockSpec(a.shape, lambda i, *_: (0,) * a.ndim)
    return pl.pallas_call(
        _tail_kernel,
        grid_spec=pltpu.PrefetchScalarGridSpec(
            num_scalar_prefetch=4,
            grid=(T // tc,),
            in_specs=[pl.BlockSpec(memory_space=pl.ANY), row(LANES), row(D_MODEL), row(PLE_DIM),
                      full(npl), full(wpg), full(wpp), full(nfin)],
            out_specs=row(D_MODEL),
            scratch_shapes=[pltpu.VMEM((2, _TILE_ROWS, D_MODEL), BF16), pltpu.SemaphoreType.DMA((2, _N_CHUNK))],
        ),
        out_shape=jax.ShapeDtypeStruct((T, D_MODEL), F32),
        compiler_params=_cp(("arbitrary",)),
        name="moe_combine_tail",
    )(run_dst, run_len, run_off, nchunk, y_pad, rt, h1, p2, npl, wpg, wpp, nfin)


def _layer(h, p_l, norm_mix, w_in, b_forget, w_attn_out, conv_w, conv_b, conv_ln_g, conv_ln_b, w_conv_out, w_o,
           norm_ffn, w_router, b_router, w_gate, b_gate, w_up, b_up, w_down, b_down, norm_ple, w_ple_gate,
           w_ple_proj):
    B, S, D = h.shape
    T = B * S
    row2 = lambda a: a.reshape(1, -1)

    o_f = 3 * ATTN_WIDTH
    o_conv = o_f + N_HEADS
    reps = LANES // N_HEADS
    w_f = jnp.tile(w_in[:, o_f:o_conv], (1, reps))
    w_all = jnp.concatenate([w_in[:, :ATTN_WIDTH] * (HEAD_DIM ** -0.5), w_in[:, ATTN_WIDTH:o_f],
                             w_in[:, o_conv:], w_f], axis=1).astype(BF16)
    bf = jnp.tile(b_forget, reps).reshape(1, LANES)

    cw = jnp.concatenate([conv_w, jnp.zeros((CONV_HALO - CONV_K, CONV_WIDTH), F32)], axis=0)
    qa, ka, va, act, sga, sgc = _inproj(h, row2(norm_mix), w_all, bf, *_bias_routing(), cw, row2(conv_b),
                                        row2(conv_ln_g), row2(conv_ln_b))
    attn = _attention(qa, ka, va)

    wr = jnp.concatenate([w_router, jnp.zeros((D, LANES - N_EXPERTS), F32)], axis=1).astype(BF16)
    br = jnp.concatenate([b_router, jnp.full((LANES - N_EXPERTS,), NEG_INF, F32)]).reshape(1, LANES)
    h1, xn2, rt, tile_cnt, tile_before, counts = _mix(
        act, attn, sga, sgc, h, w_attn_out.astype(BF16), w_conv_out.astype(BF16),
        w_o.astype(BF16), row2(norm_ffn), wr, br)
    rt = rt.reshape(T, LANES)

    cnt = counts[0, :N_EXPERTS].astype(jnp.int32)
    padded = ((cnt + MOE_BLK - 1) // MOE_BLK) * MOE_BLK
    pad_end = jnp.cumsum(padded)
    pad_start = pad_end - padded
    max_rows = T * TOP_K + (T // MOE_TILE) * N_EXPERTS * (RUN_ALIGN - 1)
    n_blocks = -(-max_rows // MOE_BLK) + N_EXPERTS
    n_rows = n_blocks * MOE_BLK
    n_used = pad_end[-1] // MOE_BLK
    run_len = tile_cnt[:, 0, :N_EXPERTS].astype(jnp.int32)
    run_off = jnp.cumsum(run_len, axis=1) - run_len
    run_dst = pad_start[None, :] + tile_before[:, 0, :N_EXPERTS].astype(jnp.int32)
    bits = jnp.asarray(_CHUNK_BITS, jnp.int32)
    nchunk = jnp.sum((run_len[:, :, None] >> bits) & 1, axis=1).astype(jnp.int32)
    runs = (run_dst.reshape(-1), run_len.reshape(-1), run_off.reshape(-1), nchunk.reshape(-1))
    blk = jnp.arange(n_blocks, dtype=jnp.int32)
    block_expert = jnp.minimum(
        jnp.sum((pad_end[None, :] <= (blk * MOE_BLK)[:, None]).astype(jnp.int32), axis=1), N_EXPERTS - 1)
    fill_rows = jnp.where(cnt > 0, pad_end - MOE_BLK, n_used * MOE_BLK).astype(jnp.int32)

    nused = n_used.astype(jnp.int32).reshape(1)
    first = (blk < n_used) & ((blk == 0) | (block_expert != jnp.roll(block_expert, 1)))
    wslot = (jnp.cumsum(first.astype(jnp.int32)) - 1) % 2
    later_first = jnp.where(first[None, :] & (blk[None, :] > blk[:, None]), blk[None, :], n_blocks)
    next_pos = jnp.min(later_first, axis=1)
    next_expert = jnp.sum(jnp.where(blk[None, :] == next_pos[:, None], block_expert[None, :], 0), axis=1)
    next_expert = jnp.where(next_pos < n_blocks, next_expert, -1).astype(jnp.int32)

    x_pad = _dispatch(fill_rows, nused, *runs, rt, xn2.reshape(T, D), n_rows)
    y_pad = _experts(block_expert, nused, first.astype(jnp.int32), next_expert, wslot.astype(jnp.int32), x_pad,
                     w_gate, b_gate.reshape(N_EXPERTS, 1, D), w_up,
                     b_up.reshape(N_EXPERTS, 1, D), w_down, b_down.reshape(N_EXPERTS, 1, D))
    return _tail, (*runs, y_pad, rt, h1.reshape(T, D),
                   p_l.reshape(T, PLE_DIM), row2(norm_ple), w_ple_gate.astype(BF16), w_ple_proj.astype(BF16))


def kernel(x, p, norm_mix, w_in, b_forget, w_attn_out, conv_w, conv_b, conv_ln_g, conv_ln_b, w_conv_out, w_o,
           norm_ffn, w_router, b_router, w_gate, b_gate, w_up, b_up, w_down, b_down, norm_ple, w_ple_gate,
           w_ple_proj, norm_final):
    B, S, D = x.shape
    assert p.shape[0] == 1 and D == D_MODEL
    tail, args = _layer(x, p[0], norm_mix[0], w_in[0], b_forget[0], w_attn_out[0], conv_w[0], conv_b[0],
                        conv_ln_g[0], conv_ln_b[0], w_conv_out[0], w_o[0], norm_ffn[0], w_router[0],
                        b_router[0], w_gate[0], b_gate[0], w_up[0], b_up[0], w_down[0], b_down[0],
                        norm_ple[0], w_ple_gate[0], w_ple_proj[0])
    return tail(*args, norm_final.reshape(1, D)).reshape(B, S, D)
```

```python
import functools

import jax
import jax.numpy as jnp
import numpy as np
from jax import lax
from jax.experimental import pallas as pl
from jax.experimental.pallas import tpu as pltpu

F32 = jnp.float32
BF16 = jnp.bfloat16

D_MODEL = 1024
HEAD_DIM = 64
N_HEADS = 8
ATTN_WIDTH = N_HEADS * HEAD_DIM
CONV_WIDTH = 512
CONV_K = 31
N_EXPERTS = 32
TOP_K = 4
SWIGLU_LIMIT = 7.0
SWIGLU_ALPHA = 1.702
PLE_DIM = 256
RMS_EPS = 1e-6
LN_EPS = 1e-5
NEG_INF = -1e30

LANES = 128
SUBLANES = 8
VMEM_LIMIT = 56 * 1024 * 1024

ROW_TILE = 512
ATTN_T = 512
HEAD_PAD = 128
CONV_HALO = 32
MOE_BLK = 512
MOE_TILE = 512
RUN_ALIGN = 16


def _cp(sem):
    return pltpu.CompilerParams(dimension_semantics=sem, vmem_limit_bytes=VMEM_LIMIT)


def _dot(a, b):
    return jnp.dot(a, b, preferred_element_type=F32)


def _sigmoid(x):
    return 1.0 / (1.0 + jnp.exp(-x))


def _rms(x, g):
    return x * lax.rsqrt(jnp.mean(x * x, axis=-1, keepdims=True) + RMS_EPS) * g


_C_QKV = 3 * ATTN_WIDTH
_C_CONV = _C_QKV + 2 * CONV_WIDTH
_C_GA = _C_CONV + D_MODEL
_C_GC = _C_GA + D_MODEL
_C_END = _C_GC + LANES


_HP_ALL = N_HEADS * HEAD_PAD
_N_CSPLIT = 3


def _split3(x):
    hi = x.astype(BF16)
    r1 = x - hi.astype(F32)
    mid = r1.astype(BF16)
    lo = (r1 - mid.astype(F32)).astype(BF16)
    return hi, mid, lo


def _store_heads(o_ref, src, aux):
    low = lax.broadcasted_iota(jnp.int32, (src.shape[0], LANES), 1) < HEAD_DIM
    for h in range(N_HEADS):
        chunk = src[:, (h // 2) * LANES:(h // 2 + 1) * LANES]
        if h % 2:
            chunk = pltpu.roll(chunk, HEAD_DIM, 1)
        o_ref[0, :, h * HEAD_PAD:(h + 1) * HEAD_PAD] = jnp.where(
            low, chunk, aux[:, h * HEAD_PAD:(h + 1) * HEAD_PAD]).astype(BF16)


def _conv_act(z, zs_ref, ph_ref, cv_ref, cw_ref, cb_ref, lng_ref, lnb_ref):
    ts = z.shape[0]
    zs_ref[0:CONV_HALO, :] = zs_ref[ts:ts + CONV_HALO, :]
    zs_ref[CONV_HALO:CONV_HALO + ts, :] = z
    base = CONV_HALO - (CONV_K - 1)
    for ph in range(SUBLANES):
        rows = ts + SUBLANES * ((CONV_K - 1 - ph) // SUBLANES)
        ph_ref[ph, 0:rows, :] = zs_ref[base + ph:base + ph + rows, :]
    rc = 128
    for c0 in range(0, CONV_WIDTH, LANES):
        for r0 in range(0, ts, rc):
            acc = jnp.zeros((rc, LANES), F32)
            for j in range(CONV_K):
                a0 = r0 + SUBLANES * (j // SUBLANES)
                acc = acc + cw_ref[j:j + 1, c0:c0 + LANES] * ph_ref[j % SUBLANES, a0:a0 + rc, c0:c0 + LANES]
            cv_ref[r0:r0 + rc, c0:c0 + LANES] = acc
    cv = cv_ref[...] + cb_ref[...]
    mu = jnp.mean(cv, axis=-1, keepdims=True)
    d = cv - mu
    var = jnp.mean(d * d, axis=-1, keepdims=True)
    y = d * lax.rsqrt(var + LN_EPS) * lng_ref[...] + lnb_ref[...]
    return (y * _sigmoid(y)).astype(BF16)


def _inproj_kernel(x_ref, g_ref, w_ref, bf_ref, selq_ref, selk_ref, auxq_ref, auxk_ref, auxv_ref,
                   cw_ref, cb_ref, lng_ref, lnb_ref, qa_ref, ka_ref, va_ref, act_ref, sga_ref, sgc_ref,
                   carry_ref, zs_ref, ph_ref, cv_ref):
    tm = x_ref.shape[1]

    @pl.when(pl.program_id(1) == 0)
    def _():
        carry_ref[...] = jnp.zeros_like(carry_ref)
        zs_ref[tm:tm + CONV_HALO, :] = jnp.zeros((CONV_HALO, CONV_WIDTH), F32)

    xn = _rms(x_ref[0], g_ref[...]).astype(BF16)

    f = _dot(xn, w_ref[:, _C_GC:_C_END]) + bf_ref[...]
    lf = jnp.minimum(f, 0.0) - jnp.log(1.0 + jnp.exp(-jnp.abs(f)))
    r = lax.broadcasted_iota(jnp.int32, (tm, tm), 0)
    c = lax.broadcasted_iota(jnp.int32, (tm, tm), 1)
    tri = jnp.where(c <= r, 1.0, 0.0).astype(BF16)
    hi, mid, lo = _split3(lf)
    cs = _dot(tri, hi) + _dot(tri, mid) + _dot(tri, lo) + carry_ref[...]
    carry_ref[...] = cs[tm - 1:tm, :]
    chi, cmid, clo = (a.astype(F32) for a in _split3(cs))
    group = lax.broadcasted_iota(jnp.int32, (tm, LANES), 1) // N_HEADS
    csel = jnp.where(group == 0, chi, jnp.where(group == 1, cmid, jnp.where(
        group == 2, clo, jnp.where(group == 3, -chi, jnp.where(group == 4, -cmid, -clo))))).astype(BF16)
    aux_q = _dot(csel, selq_ref[...]) + auxq_ref[...]
    aux_k = _dot(csel, selk_ref[...]) + auxk_ref[...]

    qkv = _dot(xn, w_ref[:, 0:_C_QKV])
    _store_heads(qa_ref, qkv[:, 0:ATTN_WIDTH], aux_q)
    _store_heads(ka_ref, qkv[:, ATTN_WIDTH:2 * ATTN_WIDTH], aux_k)
    _store_heads(va_ref, qkv[:, 2 * ATTN_WIDTH:3 * ATTN_WIDTH], auxv_ref[...])
    u = _dot(xn, w_ref[:, _C_QKV:_C_CONV])
    z = u[:, 0:CONV_WIDTH] * _sigmoid(u[:, CONV_WIDTH:])
    act_ref[0] = _conv_act(z, zs_ref, ph_ref, cv_ref, cw_ref, cb_ref, lng_ref, lnb_ref)
    sga_ref[0] = _sigmoid(_dot(xn, w_ref[:, _C_CONV:_C_GA])).astype(BF16)
    sgc_ref[0] = _sigmoid(_dot(xn, w_ref[:, _C_GA:_C_GC])).astype(BF16)


def _inproj(x, g, w, bf, selq, selk, auxq, auxk, auxv, cw, cb, lng, lnb):
    B, S, _ = x.shape
    tm = ROW_TILE
    row = lambda n: pl.BlockSpec((1, tm, n), lambda b, i: (b, i, 0))
    full = lambda a: pl.BlockSpec(a.shape, lambda b, i: (0,) * a.ndim, pipeline_mode=pl.Buffered(1))
    sds = lambda n, dt: jax.ShapeDtypeStruct((B, S, n), dt)
    consts = (g, w, bf, selq, selk, auxq, auxk, auxv, cw, cb, lng, lnb)
    return pl.pallas_call(
        _inproj_kernel,
        grid=(B, S // tm),
        in_specs=[row(D_MODEL)] + [full(a) for a in consts],
        out_specs=[row(_HP_ALL), row(_HP_ALL), row(_HP_ALL), row(CONV_WIDTH), row(D_MODEL), row(D_MODEL)],
        out_shape=[sds(_HP_ALL, BF16)] * 3 + [sds(CONV_WIDTH, BF16), sds(D_MODEL, BF16), sds(D_MODEL, BF16)],
        scratch_shapes=[pltpu.VMEM((1, LANES), F32), pltpu.VMEM((CONV_HALO + tm, CONV_WIDTH), F32),
                        pltpu.VMEM((SUBLANES, tm + CONV_HALO - SUBLANES, CONV_WIDTH), F32),
                        pltpu.VMEM((tm, CONV_WIDTH), F32)],
        compiler_params=_cp(("parallel", "arbitrary")),
        name="inproj",
    )(x, g, w, bf, selq, selk, auxq, auxk, auxv, cw, cb, lng, lnb)


def _bias_routing():
    selq = np.zeros((LANES, _HP_ALL), np.float32)
    selk = np.zeros((LANES, _HP_ALL), np.float32)
    auxq = np.zeros((1, _HP_ALL), np.float32)
    auxk = np.zeros((1, _HP_ALL), np.float32)
    auxv = np.zeros((1, _HP_ALL), np.float32)
    for h in range(N_HEADS):
        base = h * HEAD_PAD + HEAD_DIM
        for g in range(_N_CSPLIT):
            selq[g * N_HEADS + h, base + g] = 1.0
            auxk[0, base + g] = 1.0
            auxq[0, base + _N_CSPLIT + g] = 1.0
            selk[(_N_CSPLIT + g) * N_HEADS + h, base + _N_CSPLIT + g] = 1.0
        auxv[0, base] = 1.0
    return (jnp.asarray(selq, BF16), jnp.asarray(selk, BF16), jnp.asarray(auxq), jnp.asarray(auxk),
            jnp.asarray(auxv))


_HEADS_PER_STEP = 4


def _attn_kernel(qa_ref, ka_ref, va_ref, o_ref, s_ref):
    i = pl.program_id(2)
    t = ATTN_T
    dn = (((1,), (1,)), ((), ()))
    nc = t // LANES

    def scores(j, slot):
        r0 = pl.multiple_of(j * t, t)
        for hh in range(_HEADS_PER_STEP):
            lanes = slice(hh * HEAD_PAD, (hh + 1) * HEAD_PAD)
            s_ref[slot, hh] = lax.dot_general(qa_ref[0, :, lanes], ka_ref[0, pl.ds(r0, t), lanes], dn,
                                              preferred_element_type=F32)

    def consume(j, slot, carry, masked):
        r0 = pl.multiple_of(j * t, t)
        new = []
        for hh in range(_HEADS_PER_STEP):
            m, acc = carry[hh]
            lanes = slice(hh * HEAD_PAD, (hh + 1) * HEAD_PAD)
            s = s_ref[slot, hh]
            if masked:
                row = lax.broadcasted_iota(jnp.int32, (t, t), 0)
                col = lax.broadcasted_iota(jnp.int32, (t, t), 1)
                s = jnp.where(col <= row, s, NEG_INF)
            sc = [s[:, c * LANES:(c + 1) * LANES] for c in range(nc)]
            mloc = functools.reduce(jnp.maximum, sc)
            m_new = jnp.maximum(m, jnp.max(mloc, axis=-1, keepdims=True))
            p = jnp.concatenate([jnp.exp(x - m_new).astype(BF16) for x in sc], axis=1)
            acc = jnp.exp(m - m_new) * acc + _dot(p, va_ref[0, pl.ds(r0, t), lanes])
            new.append((m_new, acc))
        return tuple(new)

    def finish(carry):
        outs = [acc[:, 0:HEAD_DIM] / acc[:, HEAD_DIM:HEAD_DIM + 1] for _, acc in carry]
        o_ref[0] = jnp.concatenate(outs, axis=-1).astype(BF16)

    def pair(pp, carry):
        j = 2 * pp
        scores(j + 1, 1)
        carry = consume(j, 0, carry, False)
        scores(j + 2, 0)
        return consume(j + 1, 1, carry, False)

    init = tuple((jnp.full((t, LANES), NEG_INF, F32), jnp.zeros((t, HEAD_PAD), F32))
                 for _ in range(_HEADS_PER_STEP))
    scores(0, 0)
    carry = lax.fori_loop(0, i // 2, pair, init)

    @pl.when(i % 2 == 0)
    def _():
        finish(consume(i, 0, carry, True))

    @pl.when(i % 2 == 1)
    def _():
        scores(i, 1)
        finish(consume(i, 1, consume(i - 1, 0, carry, False), True))


def _attention(qa, ka, va):
    B, S, _ = qa.shape
    t = ATTN_T
    w = _HEADS_PER_STEP * HEAD_PAD
    return pl.pallas_call(
        _attn_kernel,
        grid=(B, N_HEADS // _HEADS_PER_STEP, S // t),
        in_specs=[pl.BlockSpec((1, t, w), lambda b, h, i: (b, i, h)),
                  pl.BlockSpec((1, S, w), lambda b, h, i: (b, 0, h)),
                  pl.BlockSpec((1, S, w), lambda b, h, i: (b, 0, h))],
        out_specs=pl.BlockSpec((1, t, _HEADS_PER_STEP * HEAD_DIM), lambda b, h, i: (b, i, h)),
        out_shape=jax.ShapeDtypeStruct((B, S, ATTN_WIDTH), BF16),
        scratch_shapes=[pltpu.VMEM((2, _HEADS_PER_STEP, t, t), F32)],
        compiler_params=_cp(("parallel", "parallel", "arbitrary")),
        name="fox_attention",
    )(qa, ka, va)


def _mix_kernel(act_ref, attn_ref, sga_ref, sgc_ref, x_ref, wa_ref, wc_ref, wo_ref, nf_ref, wr_ref, br_ref,
                h_ref, xn_ref, rt_ref, tcnt_ref, tbef_ref, cnt_ref, carry_ref):
    i = pl.program_id(1)
    branch_c = _dot(act_ref[0], wc_ref[...])
    branch_a = _dot(attn_ref[0], wa_ref[...])
    merged = sga_ref[0].astype(F32) * branch_a + sgc_ref[0].astype(F32) * branch_c
    h = x_ref[0] + _dot(merged.astype(BF16), wo_ref[...])
    h_ref[0] = h
    xn = _rms(h, nf_ref[...]).astype(BF16)
    xn_ref[0] = xn

    @pl.when((pl.program_id(0) == 0) & (i == 0))
    def _():
        carry_ref[...] = jnp.zeros_like(carry_ref)

    rt, n_tile = _route_tile(_dot(xn, wr_ref[...]) + br_ref[...])
    rt_ref[0] = rt
    tcnt_ref[0] = n_tile
    tbef_ref[0] = carry_ref[...]
    total = carry_ref[...] + n_tile
    carry_ref[...] = total
    cnt_ref[...] = total


def _mix(act, attn, sga, sgc, x, wa, wc, wo, nf, wr, br):
    B, S, _ = x.shape
    ts = ROW_TILE
    assert ts == MOE_TILE, "routing runs on the mix kernel's row tiles"
    nt = S // ts
    row = lambda n: pl.BlockSpec((1, ts, n), lambda b, i: (b, i, 0))
    full = lambda a: pl.BlockSpec(a.shape, lambda b, i: (0,) * a.ndim)
    per_tile = pl.BlockSpec((1, 1, LANES), lambda b, i: (b * nt + i, 0, 0))
    return pl.pallas_call(
        _mix_kernel,
        grid=(B, nt),
        in_specs=[row(CONV_WIDTH), row(ATTN_WIDTH), row(D_MODEL), row(D_MODEL), row(D_MODEL),
                  full(wa), full(wc), full(wo), full(nf), full(wr), full(br)],
        out_specs=[row(D_MODEL), row(D_MODEL), row(LANES), per_tile, per_tile,
                   pl.BlockSpec((1, LANES), lambda b, i: (0, 0))],
        out_shape=[jax.ShapeDtypeStruct((B, S, D_MODEL), F32), jax.ShapeDtypeStruct((B, S, D_MODEL), BF16),
                   jax.ShapeDtypeStruct((B, S, LANES), F32), jax.ShapeDtypeStruct((B * nt, 1, LANES), F32),
                   jax.ShapeDtypeStruct((B * nt, 1, LANES), F32), jax.ShapeDtypeStruct((1, LANES), F32)],
        scratch_shapes=[pltpu.VMEM((1, LANES), F32)],
        compiler_params=_cp(("arbitrary", "arbitrary")),
        name="mix_merge",
    )(act, attn, sga, sgc, x, wa, wc, wo, nf, wr, br)


def _route_tile(vals):
    tr = vals.shape[0]
    lane = lax.broadcasted_iota(jnp.int32, (tr, LANES), 1).astype(F32)
    top_v, top_i, hots = [], [], []
    for _ in range(TOP_K):
        m = jnp.max(vals, axis=-1, keepdims=True)
        idx = jnp.min(jnp.where(vals == m, lane, float(LANES)), axis=-1, keepdims=True)
        hot = lane == idx
        vals = jnp.where(hot, -jnp.inf, vals)
        top_v.append(m)
        top_i.append(idx)
        hots.append(hot)
    ex = [jnp.exp(v - top_v[0]) for v in top_v]
    den = ex[0] + ex[1] + ex[2] + ex[3]
    onehot = jnp.zeros((tr, LANES), F32)
    for hot in hots:
        onehot = onehot + jnp.where(hot, 1.0, 0.0)
    r = lax.broadcasted_iota(jnp.int32, (tr, tr), 0)
    c = lax.broadcasted_iota(jnp.int32, (tr, tr), 1)
    strict = jnp.where(c < r, 1.0, 0.0).astype(BF16)
    earlier = _dot(strict, onehot.astype(BF16))
    n_tile = jnp.sum(onehot, axis=0, keepdims=True)
    n_tile = jnp.floor((n_tile + (RUN_ALIGN - 1)) * (1.0 / RUN_ALIGN)) * RUN_ALIGN
    li = lax.broadcasted_iota(jnp.int32, (LANES, LANES), 0)
    lj = lax.broadcasted_iota(jnp.int32, (LANES, LANES), 1)
    lower = jnp.where(li < lj, 1.0, 0.0).astype(BF16)
    run_off = _dot(jnp.broadcast_to(n_tile, (8, LANES)).astype(BF16), lower)[0:1, :]
    within = earlier + run_off
    out = jnp.zeros((tr, LANES), F32)
    for k in range(TOP_K):
        pos = jnp.sum(jnp.where(hots[k], within, 0.0), axis=-1, keepdims=True)
        out = jnp.where(lane == k, top_i[k], out)
        out = jnp.where(lane == TOP_K + k, pos, out)
        out = jnp.where(lane == 2 * TOP_K + k, ex[k] / den, out)
    return out, n_tile


_ALIGN_BITS = RUN_ALIGN.bit_length() - 1
_TILE_ROWS = MOE_TILE * TOP_K + N_EXPERTS * RUN_ALIGN
_CHUNK_BITS = tuple(range(_ALIGN_BITS, MOE_TILE.bit_length()))
_N_CHUNK = len(_CHUNK_BITS)


def _run_copies(tile, slot, chunk_ref, nchunk_ref, tile_buf, hbm_ref, sems, to_hbm):
    for s, b in enumerate(_CHUNK_BITS):
        size = 1 << b

        def one_chunk(k, c):
            off = chunk_ref[0, 0, (2 * s) * N_EXPERTS + k]
            dst = chunk_ref[0, 0, (2 * s + 1) * N_EXPERTS + k]
            loc = tile_buf.at[slot, pl.ds(pl.multiple_of(off, RUN_ALIGN), size)]
            far = hbm_ref.at[pl.ds(pl.multiple_of(dst, RUN_ALIGN), size)]
            src, dstn = (loc, far) if to_hbm else (far, loc)
            pltpu.make_async_copy(src, dstn, sems.at[slot, s]).start(priority=s % 2)
            return c

        lax.fori_loop(0, nchunk_ref[tile * _N_CHUNK + s], one_chunk, 0)


def _run_waits(tile, slot, nchunk_ref, tile_buf, hbm_ref, sems, to_hbm):
    for s, b in enumerate(_CHUNK_BITS):
        size = 1 << b
        loc = tile_buf.at[slot, pl.ds(0, size)]
        far = hbm_ref.at[pl.ds(0, size)]
        src, dstn = (loc, far) if to_hbm else (far, loc)
        cp = pltpu.make_async_copy(src, dstn, sems.at[slot, s])
        lax.fori_loop(0, nchunk_ref[tile * _N_CHUNK + s], lambda n, c: (cp.wait(), c)[1], 0)


def _dispatch_kernel(tail_ref, nused_ref, nchunk_ref, chunk_ref,
                     rt_ref, x_ref, o_ref, xs_ref, zero_ref, sems, fill_sem, trail_sem):
    i = pl.program_id(0)
    slot = i % 2
    n_blocks = o_ref.shape[0] // MOE_BLK

    def fill(row, sem):
        return pltpu.make_async_copy(zero_ref, o_ref.at[pl.ds(pl.multiple_of(row, MOE_BLK), MOE_BLK)], sem)

    def trailing(op):
        lax.fori_loop(nused_ref[0], n_blocks, lambda n, c: (op(fill(n * MOE_BLK, trail_sem)), c)[1], 0)

    @pl.when(i == 0)
    def _():
        zero_ref[...] = jnp.zeros_like(zero_ref)
        lax.fori_loop(0, N_EXPERTS, lambda n, c: (fill(tail_ref[n], fill_sem).start(), c)[1], 0)
        trailing(lambda cp: cp.start())
        lax.fori_loop(0, N_EXPERTS, lambda n, c: (fill(tail_ref[n], fill_sem).wait(), c)[1], 0)

    pos_t = rt_ref[...].T
    xb = x_ref[...]
    for r0 in range(0, _TILE_ROWS, MOE_TILE):
        rows = (lax.broadcasted_iota(jnp.int32, (MOE_TILE, MOE_TILE), 0) + r0).astype(F32)
        place = jnp.zeros((MOE_TILE, MOE_TILE), F32)
        for k in range(TOP_K):
            place = jnp.where(rows == pos_t[TOP_K + k:TOP_K + k + 1, :], 1.0, place)
        xs_ref[slot, r0:r0 + MOE_TILE, :] = _dot(place.astype(BF16), xb).astype(BF16)

    _run_copies(i, slot, chunk_ref, nchunk_ref, xs_ref, o_ref, sems, True)

    @pl.when(i > 0)
    def _():
        _run_waits(i - 1, 1 - slot, nchunk_ref, xs_ref, o_ref, sems, True)

    @pl.when(i == pl.num_programs(0) - 1)
    def _():
        _run_waits(i, slot, nchunk_ref, xs_ref, o_ref, sems, True)
        trailing(lambda cp: cp.wait())


def _chunk_spec(index):
    return pl.BlockSpec((1, 1, 2 * _N_CHUNK * N_EXPERTS), lambda i, *_: (index(i), 0, 0), memory_space=pltpu.SMEM)


def _dispatch(tail_rows, nused, nchunk, chunks, rt, xn, n_rows):
    T = xn.shape[0]
    tb = MOE_TILE
    return pl.pallas_call(
        _dispatch_kernel,
        grid_spec=pltpu.PrefetchScalarGridSpec(
            num_scalar_prefetch=3,
            grid=(T // tb,),
            in_specs=[_chunk_spec(lambda i: i), pl.BlockSpec((tb, LANES), lambda i, *_: (i, 0)),
                      pl.BlockSpec((tb, D_MODEL), lambda i, *_: (i, 0))],
            out_specs=pl.BlockSpec(memory_space=pl.ANY),
            scratch_shapes=[pltpu.VMEM((2, _TILE_ROWS, D_MODEL), BF16), pltpu.VMEM((MOE_BLK, D_MODEL), BF16),
                            pltpu.SemaphoreType.DMA((2, _N_CHUNK)), pltpu.SemaphoreType.DMA(()),
                            pltpu.SemaphoreType.DMA(())],
        ),
        out_shape=jax.ShapeDtypeStruct((n_rows, D_MODEL), BF16),
        compiler_params=_cp(("arbitrary",)),
        name="moe_dispatch",
    )(tail_rows, nused, nchunk, chunks, rt, xn)


def _expert_kernel(be_ref, nused_ref, first_ref, next_ref, wslot_ref, x_ref, wg_hbm, bg_ref, wu_hbm, bu_ref,
                   wd_hbm, bd_ref, y_ref, stage_ref, wb_ref, sems):
    i = pl.program_id(0)

    def fetch(e, slot):
        return [pltpu.make_async_copy(w.at[e], stage_ref.at[slot, m], sems.at[slot])
                for m, w in enumerate((wg_hbm, wu_hbm, wd_hbm))]

    @pl.when(first_ref[i] != 0)
    def _():
        slot = wslot_ref[i]

        @pl.when(i == 0)
        def _():
            for cp in fetch(be_ref[0], slot):
                cp.start()

        for cp in fetch(be_ref[i], slot):
            cp.wait()
        for m in range(3):
            wb_ref[m] = stage_ref[slot, m].astype(BF16)

        @pl.when(next_ref[i] >= 0)
        def _():
            for cp in fetch(next_ref[i], 1 - slot):
                cp.start()

    @pl.when(i < nused_ref[0])
    def _():
        xb = x_ref[...]
        g = _dot(xb, wb_ref[0]) + bg_ref[0]
        u = _dot(xb, wb_ref[1]) + bu_ref[0]
        g = jnp.minimum(g, SWIGLU_LIMIT)
        u = jnp.clip(u, -SWIGLU_LIMIT, SWIGLU_LIMIT)
        glu = g * _sigmoid(SWIGLU_ALPHA * g)
        y_ref[...] = (_dot(((u + 1.0) * glu).astype(BF16), wb_ref[2]) + bd_ref[0]).astype(BF16)

    @pl.when(i >= nused_ref[0])
    def _():
        y_ref[...] = jnp.zeros_like(y_ref)


def _experts(block_expert, nused, first, next_expert, wslot, x_pad, wg, bg, wu, bu, wd, bd):
    n_rows = x_pad.shape[0]
    wspec = pl.BlockSpec(memory_space=pl.ANY)
    bspec = pl.BlockSpec((1, 1, D_MODEL), lambda i, be, *_: (be[i], 0, 0))
    yspec = pl.BlockSpec((MOE_BLK, D_MODEL), lambda i, *_: (i, 0))
    xspec = pl.BlockSpec((MOE_BLK, D_MODEL), lambda i, be, nu, *_: (jnp.minimum(i, nu[0] - 1), 0))
    return pl.pallas_call(
        _expert_kernel,
        grid_spec=pltpu.PrefetchScalarGridSpec(
            num_scalar_prefetch=5,
            grid=(n_rows // MOE_BLK,),
            in_specs=[xspec, wspec, bspec, wspec, bspec, wspec, bspec],
            out_specs=yspec,
            scratch_shapes=[pltpu.VMEM((2, 3, D_MODEL, D_MODEL), F32), pltpu.VMEM((3, D_MODEL, D_MODEL), BF16),
                            pltpu.SemaphoreType.DMA((2,))],
        ),
        out_shape=jax.ShapeDtypeStruct((n_rows, D_MODEL), BF16),
        compiler_params=_cp(("arbitrary",)),
        name="moe_experts",
    )(block_expert, nused, first, next_expert, wslot, x_pad, wg, bg, wu, bu, wd, bd)


def _tail_kernel(nchunk_ref, chunk_ref, chunk_next_ref, y_ref, rt_ref, h_ref, p_ref, npl_ref,
                 wpg_ref, wpp_ref, nfin_ref, o_ref, ys_ref, sems):
    i = pl.program_id(0)
    slot = i % 2
    fetch = functools.partial(_run_copies, nchunk_ref=nchunk_ref, tile_buf=ys_ref, hbm_ref=y_ref, sems=sems,
                              to_hbm=False)

    @pl.when(i == 0)
    def _():
        ys_ref[...] = jnp.zeros_like(ys_ref)
        fetch(i, slot, chunk_ref)

    @pl.when(i + 1 < pl.num_programs(0))
    def _():
        fetch(i + 1, 1 - slot, chunk_next_ref)

    _run_waits(i, slot, nchunk_ref, ys_ref, y_ref, sems, False)

    rt = rt_ref[...]
    cols = lax.broadcasted_iota(jnp.int32, (MOE_TILE, _TILE_ROWS), 1).astype(F32)
    weight = jnp.zeros((MOE_TILE, _TILE_ROWS), F32)
    for k in range(TOP_K):
        weight = jnp.where(cols == rt[:, TOP_K + k:TOP_K + k + 1],
                           rt[:, 2 * TOP_K + k:2 * TOP_K + k + 1], weight)
    h = h_ref[...] + _dot(weight.astype(BF16), ys_ref[slot])
    pg = _sigmoid(_dot(_rms(h, npl_ref[...]).astype(BF16), wpg_ref[...]))
    h = h + pg * _dot(p_ref[...].astype(BF16), wpp_ref[...])
    o_ref[...] = _rms(h, nfin_ref[...])


def _tail(nchunk, chunks, y_pad, rt, h1, p2, npl, wpg, wpp, nfin):
    T = h1.shape[0]
    tc = MOE_TILE
    last = T // tc - 1
    row = lambda n: pl.BlockSpec((tc, n), lambda i, *_: (i, 0))
    full = lambda a: pl.BlockSpec(a.shape, lambda i, *_: (0,) * a.ndim)
    return pl.pallas_call(
        _tail_kernel,
        grid_spec=pltpu.PrefetchScalarGridSpec(
            num_scalar_prefetch=1,
            grid=(T // tc,),
            in_specs=[_chunk_spec(lambda i: i), _chunk_spec(lambda i: jnp.minimum(i + 1, last)),
                      pl.BlockSpec(memory_space=pl.ANY), row(LANES), row(D_MODEL), row(PLE_DIM),
                      full(npl), full(wpg), full(wpp), full(nfin)],
            out_specs=row(D_MODEL),
            scratch_shapes=[pltpu.VMEM((2, _TILE_ROWS, D_MODEL), BF16), pltpu.SemaphoreType.DMA((2, _N_CHUNK))],
        ),
        out_shape=jax.ShapeDtypeStruct((T, D_MODEL), F32),
        compiler_params=_cp(("arbitrary",)),
        name="moe_combine_tail",
    )(nchunk, chunks, chunks, y_pad, rt, h1, p2, npl, wpg, wpp, nfin)


def _layer(h, p_l, norm_mix, w_in, b_forget, w_attn_out, conv_w, conv_b, conv_ln_g, conv_ln_b, w_conv_out, w_o,
           norm_ffn, w_router, b_router, w_gate, b_gate, w_up, b_up, w_down, b_down, norm_ple, w_ple_gate,
           w_ple_proj):
    B, S, D = h.shape
    T = B * S
    row2 = lambda a: a.reshape(1, -1)

    o_f = 3 * ATTN_WIDTH
    o_conv = o_f + N_HEADS
    reps = LANES // N_HEADS
    w_f = jnp.tile(w_in[:, o_f:o_conv], (1, reps))
    w_all = jnp.concatenate([w_in[:, :ATTN_WIDTH] * (HEAD_DIM ** -0.5), w_in[:, ATTN_WIDTH:o_f],
                             w_in[:, o_conv:], w_f], axis=1).astype(BF16)
    bf = jnp.tile(b_forget, reps).reshape(1, LANES)

    cw = jnp.concatenate([conv_w, jnp.zeros((CONV_HALO - CONV_K, CONV_WIDTH), F32)], axis=0)
    qa, ka, va, act, sga, sgc = _inproj(h, row2(norm_mix), w_all, bf, *_bias_routing(), cw, row2(conv_b),
                                        row2(conv_ln_g), row2(conv_ln_b))
    attn = _attention(qa, ka, va)

    wr = jnp.concatenate([w_router, jnp.zeros((D, LANES - N_EXPERTS), F32)], axis=1).astype(BF16)
    br = jnp.concatenate([b_router, jnp.full((LANES - N_EXPERTS,), NEG_INF, F32)]).reshape(1, LANES)
    h1, xn2, rt, tile_cnt, tile_before, counts = _mix(
        act, attn, sga, sgc, h, w_attn_out.astype(BF16), w_conv_out.astype(BF16),
        w_o.astype(BF16), row2(norm_ffn), wr, br)
    rt = rt.reshape(T, LANES)

    cnt = counts[0, :N_EXPERTS].astype(jnp.int32)
    padded = ((cnt + MOE_BLK - 1) // MOE_BLK) * MOE_BLK
    pad_end = jnp.cumsum(padded)
    pad_start = pad_end - padded
    max_rows = T * TOP_K + (T // MOE_TILE) * N_EXPERTS * (RUN_ALIGN - 1)
    n_blocks = -(-max_rows // MOE_BLK) + N_EXPERTS
    n_rows = n_blocks * MOE_BLK
    n_used = pad_end[-1] // MOE_BLK
    run_len = tile_cnt[:, 0, :N_EXPERTS].astype(jnp.int32)
    run_off = jnp.cumsum(run_len, axis=1) - run_len
    run_dst = pad_start[None, :] + tile_before[:, 0, :N_EXPERTS].astype(jnp.int32)
    bits = jnp.asarray(_CHUNK_BITS, jnp.int32)
    has = (run_len[:, :, None] >> bits) & 1
    done = (run_len[:, :, None] >> (bits + 1)) << (bits + 1)
    slot_in_list = jnp.cumsum(has, axis=1) - 1
    hit = (has[..., None] == 1) & (slot_in_list[..., None] == jnp.arange(N_EXPERTS, dtype=jnp.int32))
    pick = lambda v: jnp.sum(jnp.where(hit, v[..., None], 0), axis=1)
    chunks = jnp.stack([pick(run_off[:, :, None] + done), pick(run_dst[:, :, None] + done)], axis=2)
    chunks = chunks.reshape(-1, 1, 2 * _N_CHUNK * N_EXPERTS).astype(jnp.int32)
    nchunk = jnp.sum(has, axis=1).astype(jnp.int32).reshape(-1)
    blk = jnp.arange(n_blocks, dtype=jnp.int32)
    block_expert = jnp.minimum(
        jnp.sum((pad_end[None, :] <= (blk * MOE_BLK)[:, None]).astype(jnp.int32), axis=1), N_EXPERTS - 1)
    fill_rows = jnp.where(cnt > 0, pad_end - MOE_BLK, n_used * MOE_BLK).astype(jnp.int32)

    nused = n_used.astype(jnp.int32).reshape(1)
    first = (blk < n_used) & ((blk == 0) | (block_expert != jnp.roll(block_expert, 1)))
    wslot = (jnp.cumsum(first.astype(jnp.int32)) - 1) % 2
    later_first = jnp.where(first[None, :] & (blk[None, :] > blk[:, None]), blk[None, :], n_blocks)
    next_pos = jnp.min(later_first, axis=1)
    next_expert = jnp.sum(jnp.where(blk[None, :] == next_pos[:, None], block_expert[None, :], 0), axis=1)
    next_expert = jnp.where(next_pos < n_blocks, next_expert, -1).astype(jnp.int32)

    x_pad = _dispatch(fill_rows, nused, nchunk, chunks, rt, xn2.reshape(T, D), n_rows)
    y_pad = _experts(block_expert, nused, first.astype(jnp.int32), next_expert, wslot.astype(jnp.int32), x_pad,
                     w_gate, b_gate.reshape(N_EXPERTS, 1, D), w_up,
                     b_up.reshape(N_EXPERTS, 1, D), w_down, b_down.reshape(N_EXPERTS, 1, D))
    return _tail, (nchunk, chunks, y_pad, rt, h1.reshape(T, D),
                   p_l.reshape(T, PLE_DIM), row2(norm_ple), w_ple_gate.astype(BF16), w_ple_proj.astype(BF16))


def kernel(x, p, norm_mix, w_in, b_forget, w_attn_out, conv_w, conv_b, conv_ln_g, conv_ln_b, w_conv_out, w_o,
           norm_ffn, w_router, b_router, w_gate, b_gate, w_up, b_up, w_down, b_down, norm_ple, w_ple_gate,
           w_ple_proj, norm_final):
    B, S, D = x.shape
    assert p.shape[0] == 1 and D == D_MODEL
    tail, args = _layer(x, p[0], norm_mix[0], w_in[0], b_forget[0], w_attn_out[0], conv_w[0], conv_b[0],
                        conv_ln_g[0], conv_ln_b[0], w_conv_out[0], w_o[0], norm_ffn[0], w_router[0],
                        b_router[0], w_gate[0], b_gate[0], w_up[0], b_up[0], w_down[0], b_down[0],
                        norm_ple[0], w_ple_gate[0], w_ple_proj[0])
    return tail(*args, norm_final.reshape(1, D)).reshape(B, S, D)
```

```python
import functools

import jax
import jax.numpy as jnp
import numpy as np
from jax import lax
from jax.experimental import pallas as pl
from jax.experimental.pallas import tpu as pltpu

F32 = jnp.float32
BF16 = jnp.bfloat16

D_MODEL = 1024
HEAD_DIM = 64
N_HEADS = 8
ATTN_WIDTH = N_HEADS * HEAD_DIM
CONV_WIDTH = 512
CONV_K = 31
N_EXPERTS = 32
TOP_K = 4
SWIGLU_LIMIT = 7.0
SWIGLU_ALPHA = 1.702
PLE_DIM = 256
RMS_EPS = 1e-6
LN_EPS = 1e-5
NEG_INF = -1e30

LANES = 128
SUBLANES = 8
VMEM_LIMIT = 56 * 1024 * 1024

ROW_TILE = 512
ATTN_T = 512
HEAD_PAD = 128
CONV_HALO = 32
MOE_BLK = 512
MOE_TILE = 512
RUN_ALIGN = 16


def _cp(sem):
    return pltpu.CompilerParams(dimension_semantics=sem, vmem_limit_bytes=VMEM_LIMIT)


def _dot(a, b):
    return jnp.dot(a, b, preferred_element_type=F32)


def _sigmoid(x):
    return 0.5 * jnp.tanh(0.5 * x) + 0.5


def _rms(x, g):
    return x * lax.rsqrt(jnp.mean(x * x, axis=-1, keepdims=True) + RMS_EPS) * g


_C_QKV = 3 * ATTN_WIDTH
_C_CONV = _C_QKV + 2 * CONV_WIDTH
_C_GA = _C_CONV + D_MODEL
_C_GC = _C_GA + D_MODEL
_C_END = _C_GC + LANES


_HP_ALL = N_HEADS * HEAD_PAD
_N_CSPLIT = 3


def _split3(x):
    hi = x.astype(BF16)
    r1 = x - hi.astype(F32)
    mid = r1.astype(BF16)
    lo = (r1 - mid.astype(F32)).astype(BF16)
    return hi, mid, lo


def _store_heads(o_ref, src, aux):
    low = lax.broadcasted_iota(jnp.int32, (src.shape[0], LANES), 1) < HEAD_DIM
    for h in range(N_HEADS):
        chunk = src[:, (h // 2) * LANES:(h // 2 + 1) * LANES]
        if h % 2:
            chunk = pltpu.roll(chunk, HEAD_DIM, 1)
        o_ref[0, :, h * HEAD_PAD:(h + 1) * HEAD_PAD] = jnp.where(
            low, chunk, aux[:, h * HEAD_PAD:(h + 1) * HEAD_PAD]).astype(BF16)


def _conv_act(z, zs_ref, ph_ref, cv_ref, cw_ref, cb_ref, lng_ref, lnb_ref):
    ts = z.shape[0]
    zs_ref[0:CONV_HALO, :] = zs_ref[ts:ts + CONV_HALO, :]
    zs_ref[CONV_HALO:CONV_HALO + ts, :] = z
    base = CONV_HALO - (CONV_K - 1)
    for ph in range(SUBLANES):
        rows = ts + SUBLANES * ((CONV_K - 1 - ph) // SUBLANES)
        ph_ref[ph, 0:rows, :] = zs_ref[base + ph:base + ph + rows, :]
    rc = 128
    for c0 in range(0, CONV_WIDTH, LANES):
        for r0 in range(0, ts, rc):
            acc = jnp.zeros((rc, LANES), F32)
            for j in range(CONV_K):
                a0 = r0 + SUBLANES * (j // SUBLANES)
                acc = acc + cw_ref[j:j + 1, c0:c0 + LANES] * ph_ref[j % SUBLANES, a0:a0 + rc, c0:c0 + LANES]
            cv_ref[r0:r0 + rc, c0:c0 + LANES] = acc
    cv = cv_ref[...] + cb_ref[...]
    mu = jnp.mean(cv, axis=-1, keepdims=True)
    d = cv - mu
    var = jnp.mean(d * d, axis=-1, keepdims=True)
    y = d * lax.rsqrt(var + LN_EPS) * lng_ref[...] + lnb_ref[...]
    return (y * _sigmoid(y)).astype(BF16)


def _inproj_kernel(x_ref, g_ref, w_ref, bf_ref, selq_ref, selk_ref, auxq_ref, auxk_ref, auxv_ref,
                   cw_ref, cb_ref, lng_ref, lnb_ref, qa_ref, ka_ref, va_ref, act_ref, sga_ref, sgc_ref,
                   carry_ref, zs_ref, ph_ref, cv_ref):
    tm = x_ref.shape[1]

    @pl.when(pl.program_id(1) == 0)
    def _():
        carry_ref[...] = jnp.zeros_like(carry_ref)
        zs_ref[tm:tm + CONV_HALO, :] = jnp.zeros((CONV_HALO, CONV_WIDTH), F32)

    xn = _rms(x_ref[0], g_ref[...]).astype(BF16)

    f = _dot(xn, w_ref[:, _C_GC:_C_END]) + bf_ref[...]
    lf = jnp.minimum(f, 0.0) - jnp.log(1.0 + jnp.exp(-jnp.abs(f)))
    r = lax.broadcasted_iota(jnp.int32, (tm, tm), 0)
    c = lax.broadcasted_iota(jnp.int32, (tm, tm), 1)
    tri = jnp.where(c <= r, 1.0, 0.0).astype(BF16)
    hi, mid, lo = _split3(lf)
    cs = _dot(tri, hi) + _dot(tri, mid) + _dot(tri, lo) + carry_ref[...]
    carry_ref[...] = cs[tm - 1:tm, :]
    chi, cmid, clo = (a.astype(F32) for a in _split3(cs))
    group = lax.broadcasted_iota(jnp.int32, (tm, LANES), 1) // N_HEADS
    csel = jnp.where(group == 0, chi, jnp.where(group == 1, cmid, jnp.where(
        group == 2, clo, jnp.where(group == 3, -chi, jnp.where(group == 4, -cmid, -clo))))).astype(BF16)
    aux_q = _dot(csel, selq_ref[...]) + auxq_ref[...]
    aux_k = _dot(csel, selk_ref[...]) + auxk_ref[...]

    qkv = _dot(xn, w_ref[:, 0:_C_QKV])
    _store_heads(qa_ref, qkv[:, 0:ATTN_WIDTH], aux_q)
    _store_heads(ka_ref, qkv[:, ATTN_WIDTH:2 * ATTN_WIDTH], aux_k)
    _store_heads(va_ref, qkv[:, 2 * ATTN_WIDTH:3 * ATTN_WIDTH], auxv_ref[...])
    u = _dot(xn, w_ref[:, _C_QKV:_C_CONV])
    z = u[:, 0:CONV_WIDTH] * _sigmoid(u[:, CONV_WIDTH:])
    act_ref[0] = _conv_act(z, zs_ref, ph_ref, cv_ref, cw_ref, cb_ref, lng_ref, lnb_ref)
    sga_ref[0] = _sigmoid(_dot(xn, w_ref[:, _C_CONV:_C_GA])).astype(BF16)
    sgc_ref[0] = _sigmoid(_dot(xn, w_ref[:, _C_GA:_C_GC])).astype(BF16)


def _inproj(x, g, w, bf, selq, selk, auxq, auxk, auxv, cw, cb, lng, lnb):
    B, S, _ = x.shape
    tm = ROW_TILE
    row = lambda n: pl.BlockSpec((1, tm, n), lambda b, i: (b, i, 0))
    full = lambda a: pl.BlockSpec(a.shape, lambda b, i: (0,) * a.ndim, pipeline_mode=pl.Buffered(1))
    sds = lambda n, dt: jax.ShapeDtypeStruct((B, S, n), dt)
    consts = (g, w, bf, selq, selk, auxq, auxk, auxv, cw, cb, lng, lnb)
    return pl.pallas_call(
        _inproj_kernel,
        grid=(B, S // tm),
        in_specs=[row(D_MODEL)] + [full(a) for a in consts],
        out_specs=[row(_HP_ALL), row(_HP_ALL), row(_HP_ALL), row(CONV_WIDTH), row(D_MODEL), row(D_MODEL)],
        out_shape=[sds(_HP_ALL, BF16)] * 3 + [sds(CONV_WIDTH, BF16), sds(D_MODEL, BF16), sds(D_MODEL, BF16)],
        scratch_shapes=[pltpu.VMEM((1, LANES), F32), pltpu.VMEM((CONV_HALO + tm, CONV_WIDTH), F32),
                        pltpu.VMEM((SUBLANES, tm + CONV_HALO - SUBLANES, CONV_WIDTH), F32),
                        pltpu.VMEM((tm, CONV_WIDTH), F32)],
        compiler_params=_cp(("parallel", "arbitrary")),
        name="inproj",
    )(x, g, w, bf, selq, selk, auxq, auxk, auxv, cw, cb, lng, lnb)


def _bias_routing():
    selq = np.zeros((LANES, _HP_ALL), np.float32)
    selk = np.zeros((LANES, _HP_ALL), np.float32)
    auxq = np.zeros((1, _HP_ALL), np.float32)
    auxk = np.zeros((1, _HP_ALL), np.float32)
    auxv = np.zeros((1, _HP_ALL), np.float32)
    for h in range(N_HEADS):
        base = h * HEAD_PAD + HEAD_DIM
        for g in range(_N_CSPLIT):
            selq[g * N_HEADS + h, base + g] = 1.0
            auxk[0, base + g] = 1.0
            auxq[0, base + _N_CSPLIT + g] = 1.0
            selk[(_N_CSPLIT + g) * N_HEADS + h, base + _N_CSPLIT + g] = 1.0
        auxv[0, base] = 1.0
    return (jnp.asarray(selq, BF16), jnp.asarray(selk, BF16), jnp.asarray(auxq), jnp.asarray(auxk),
            jnp.asarray(auxv))


_HEADS_PER_STEP = 4


def _attn_kernel(qa_ref, ka_ref, va_ref, o_ref, s_ref):
    i = pl.program_id(2)
    t = ATTN_T
    dn = (((1,), (1,)), ((), ()))
    nc = t // LANES

    def scores(j, slot):
        r0 = pl.multiple_of(j * t, t)
        for hh in range(_HEADS_PER_STEP):
            lanes = slice(hh * HEAD_PAD, (hh + 1) * HEAD_PAD)
            s_ref[slot, hh] = lax.dot_general(qa_ref[0, :, lanes], ka_ref[0, pl.ds(r0, t), lanes], dn,
                                              preferred_element_type=F32)

    def consume(j, slot, carry, masked):
        r0 = pl.multiple_of(j * t, t)
        new = []
        for hh in range(_HEADS_PER_STEP):
            m, acc = carry[hh]
            lanes = slice(hh * HEAD_PAD, (hh + 1) * HEAD_PAD)
            s = s_ref[slot, hh]
            if masked:
                row = lax.broadcasted_iota(jnp.int32, (t, t), 0)
                col = lax.broadcasted_iota(jnp.int32, (t, t), 1)
                s = jnp.where(col <= row, s, NEG_INF)
            sc = [s[:, c * LANES:(c + 1) * LANES] for c in range(nc)]
            mloc = functools.reduce(jnp.maximum, sc)
            m_new = jnp.maximum(m, jnp.max(mloc, axis=-1, keepdims=True))
            p = jnp.concatenate([jnp.exp(x - m_new).astype(BF16) for x in sc], axis=1)
            acc = jnp.exp(m - m_new) * acc + _dot(p, va_ref[0, pl.ds(r0, t), lanes])
            new.append((m_new, acc))
        return tuple(new)

    def finish(carry):
        outs = [acc[:, 0:HEAD_DIM] / acc[:, HEAD_DIM:HEAD_DIM + 1] for _, acc in carry]
        o_ref[0] = jnp.concatenate(outs, axis=-1).astype(BF16)

    def pair(pp, carry):
        j = 2 * pp
        scores(j + 1, 1)
        carry = consume(j, 0, carry, False)
        scores(j + 2, 0)
        return consume(j + 1, 1, carry, False)

    init = tuple((jnp.full((t, LANES), NEG_INF, F32), jnp.zeros((t, HEAD_PAD), F32))
                 for _ in range(_HEADS_PER_STEP))
    scores(0, 0)
    carry = lax.fori_loop(0, i // 2, pair, init)

    @pl.when(i % 2 == 0)
    def _():
        finish(consume(i, 0, carry, True))

    @pl.when(i % 2 == 1)
    def _():
        scores(i, 1)
        finish(consume(i, 1, consume(i - 1, 0, carry, False), True))


def _attention(qa, ka, va):
    B, S, _ = qa.shape
    t = ATTN_T
    w = _HEADS_PER_STEP * HEAD_PAD
    return pl.pallas_call(
        _attn_kernel,
        grid=(B, N_HEADS // _HEADS_PER_STEP, S // t),
        in_specs=[pl.BlockSpec((1, t, w), lambda b, h, i: (b, i, h)),
                  pl.BlockSpec((1, S, w), lambda b, h, i: (b, 0, h)),
                  pl.BlockSpec((1, S, w), lambda b, h, i: (b, 0, h))],
        out_specs=pl.BlockSpec((1, t, _HEADS_PER_STEP * HEAD_DIM), lambda b, h, i: (b, i, h)),
        out_shape=jax.ShapeDtypeStruct((B, S, ATTN_WIDTH), BF16),
        scratch_shapes=[pltpu.VMEM((2, _HEADS_PER_STEP, t, t), F32)],
        compiler_params=_cp(("parallel", "parallel", "arbitrary")),
        name="fox_attention",
    )(qa, ka, va)


def _mix_kernel(act_ref, attn_ref, sga_ref, sgc_ref, x_ref, wa_ref, wc_ref, wo_ref, nf_ref, wr_ref, br_ref,
                h_ref, xn_ref, rt_ref, tcnt_ref, tbef_ref, cnt_ref, carry_ref):
    i = pl.program_id(1)
    branch_c = _dot(act_ref[0], wc_ref[...])
    branch_a = _dot(attn_ref[0], wa_ref[...])
    merged = sga_ref[0].astype(F32) * branch_a + sgc_ref[0].astype(F32) * branch_c
    h = x_ref[0] + _dot(merged.astype(BF16), wo_ref[...])
    h_ref[0] = h
    xn = _rms(h, nf_ref[...]).astype(BF16)
    xn_ref[0] = xn

    @pl.when((pl.program_id(0) == 0) & (i == 0))
    def _():
        carry_ref[...] = jnp.zeros_like(carry_ref)

    rt, n_tile = _route_tile(_dot(xn, wr_ref[...]) + br_ref[...])
    rt_ref[0] = rt
    tcnt_ref[0] = n_tile
    tbef_ref[0] = carry_ref[...]
    total = carry_ref[...] + n_tile
    carry_ref[...] = total
    cnt_ref[...] = total


def _mix(act, attn, sga, sgc, x, wa, wc, wo, nf, wr, br):
    B, S, _ = x.shape
    ts = ROW_TILE
    assert ts == MOE_TILE, "routing runs on the mix kernel's row tiles"
    nt = S // ts
    row = lambda n: pl.BlockSpec((1, ts, n), lambda b, i: (b, i, 0))
    full = lambda a: pl.BlockSpec(a.shape, lambda b, i: (0,) * a.ndim)
    per_tile = pl.BlockSpec((1, 1, LANES), lambda b, i: (b * nt + i, 0, 0))
    return pl.pallas_call(
        _mix_kernel,
        grid=(B, nt),
        in_specs=[row(CONV_WIDTH), row(ATTN_WIDTH), row(D_MODEL), row(D_MODEL), row(D_MODEL),
                  full(wa), full(wc), full(wo), full(nf), full(wr), full(br)],
        out_specs=[row(D_MODEL), row(D_MODEL), row(LANES), per_tile, per_tile,
                   pl.BlockSpec((1, LANES), lambda b, i: (0, 0))],
        out_shape=[jax.ShapeDtypeStruct((B, S, D_MODEL), F32), jax.ShapeDtypeStruct((B, S, D_MODEL), BF16),
                   jax.ShapeDtypeStruct((B, S, LANES), F32), jax.ShapeDtypeStruct((B * nt, 1, LANES), F32),
                   jax.ShapeDtypeStruct((B * nt, 1, LANES), F32), jax.ShapeDtypeStruct((1, LANES), F32)],
        scratch_shapes=[pltpu.VMEM((1, LANES), F32)],
        compiler_params=_cp(("arbitrary", "arbitrary")),
        name="mix_merge",
    )(act, attn, sga, sgc, x, wa, wc, wo, nf, wr, br)


def _route_tile(vals):
    tr = vals.shape[0]
    lane = lax.broadcasted_iota(jnp.int32, (tr, LANES), 1).astype(F32)
    top_v, top_i, hots = [], [], []
    for _ in range(TOP_K):
        m = jnp.max(vals, axis=-1, keepdims=True)
        idx = jnp.min(jnp.where(vals == m, lane, float(LANES)), axis=-1, keepdims=True)
        hot = lane == idx
        vals = jnp.where(hot, -jnp.inf, vals)
        top_v.append(m)
        top_i.append(idx)
        hots.append(hot)
    ex = [jnp.exp(v - top_v[0]) for v in top_v]
    den = ex[0] + ex[1] + ex[2] + ex[3]
    onehot = jnp.zeros((tr, LANES), F32)
    for hot in hots:
        onehot = onehot + jnp.where(hot, 1.0, 0.0)
    r = lax.broadcasted_iota(jnp.int32, (tr, tr), 0)
    c = lax.broadcasted_iota(jnp.int32, (tr, tr), 1)
    strict = jnp.where(c < r, 1.0, 0.0).astype(BF16)
    earlier = _dot(strict, onehot.astype(BF16))
    n_tile = jnp.sum(onehot, axis=0, keepdims=True)
    n_tile = jnp.floor((n_tile + (RUN_ALIGN - 1)) * (1.0 / RUN_ALIGN)) * RUN_ALIGN
    li = lax.broadcasted_iota(jnp.int32, (LANES, LANES), 0)
    lj = lax.broadcasted_iota(jnp.int32, (LANES, LANES), 1)
    lower = jnp.where(li < lj, 1.0, 0.0).astype(BF16)
    run_off = _dot(jnp.broadcast_to(n_tile, (8, LANES)).astype(BF16), lower)[0:1, :]
    within = earlier + run_off
    out = jnp.zeros((tr, LANES), F32)
    for k in range(TOP_K):
        pos = jnp.sum(jnp.where(hots[k], within, 0.0), axis=-1, keepdims=True)
        out = jnp.where(lane == k, top_i[k], out)
        out = jnp.where(lane == TOP_K + k, pos, out)
        out = jnp.where(lane == 2 * TOP_K + k, ex[k] / den, out)
    return out, n_tile


_ALIGN_BITS = RUN_ALIGN.bit_length() - 1
_TILE_ROWS = MOE_TILE * TOP_K + N_EXPERTS * RUN_ALIGN
_CHUNK_BITS = tuple(range(_ALIGN_BITS, MOE_TILE.bit_length()))
_N_CHUNK = len(_CHUNK_BITS)


def _run_copies(tile, slot, chunk_ref, nchunk_ref, tile_buf, hbm_ref, sems, to_hbm):
    for s, b in enumerate(_CHUNK_BITS):
        size = 1 << b

        def one_chunk(k, c):
            off = chunk_ref[0, 0, (2 * s) * N_EXPERTS + k]
            dst = chunk_ref[0, 0, (2 * s + 1) * N_EXPERTS + k]
            loc = tile_buf.at[slot, pl.ds(pl.multiple_of(off, RUN_ALIGN), size)]
            far = hbm_ref.at[pl.ds(pl.multiple_of(dst, RUN_ALIGN), size)]
            src, dstn = (loc, far) if to_hbm else (far, loc)
            pltpu.make_async_copy(src, dstn, sems.at[slot, s]).start(priority=s % 2)
            return c

        lax.fori_loop(0, nchunk_ref[tile * _N_CHUNK + s], one_chunk, 0)


def _run_waits(tile, slot, nchunk_ref, tile_buf, hbm_ref, sems, to_hbm):
    for s, b in enumerate(_CHUNK_BITS):
        size = 1 << b
        loc = tile_buf.at[slot, pl.ds(0, size)]
        far = hbm_ref.at[pl.ds(0, size)]
        src, dstn = (loc, far) if to_hbm else (far, loc)
        cp = pltpu.make_async_copy(src, dstn, sems.at[slot, s])
        lax.fori_loop(0, nchunk_ref[tile * _N_CHUNK + s], lambda n, c: (cp.wait(), c)[1], 0)


def _dispatch_kernel(tail_ref, nused_ref, nchunk_ref, chunk_ref,
                     rt_ref, x_ref, o_ref, xs_ref, zero_ref, sems, fill_sem, trail_sem):
    i = pl.program_id(0)
    slot = i % 2
    n_blocks = o_ref.shape[0] // MOE_BLK

    def fill(row, sem):
        return pltpu.make_async_copy(zero_ref, o_ref.at[pl.ds(pl.multiple_of(row, MOE_BLK), MOE_BLK)], sem)

    def trailing(op):
        lax.fori_loop(nused_ref[0], n_blocks, lambda n, c: (op(fill(n * MOE_BLK, trail_sem)), c)[1], 0)

    @pl.when(i == 0)
    def _():
        zero_ref[...] = jnp.zeros_like(zero_ref)
        lax.fori_loop(0, N_EXPERTS, lambda n, c: (fill(tail_ref[n], fill_sem).start(), c)[1], 0)
        trailing(lambda cp: cp.start())
        lax.fori_loop(0, N_EXPERTS, lambda n, c: (fill(tail_ref[n], fill_sem).wait(), c)[1], 0)

    pos_t = rt_ref[...].T
    xb = x_ref[...]
    for r0 in range(0, _TILE_ROWS, MOE_TILE):
        rows = (lax.broadcasted_iota(jnp.int32, (MOE_TILE, MOE_TILE), 0) + r0).astype(F32)
        place = jnp.zeros((MOE_TILE, MOE_TILE), F32)
        for k in range(TOP_K):
            place = jnp.where(rows == pos_t[TOP_K + k:TOP_K + k + 1, :], 1.0, place)
        xs_ref[slot, r0:r0 + MOE_TILE, :] = _dot(place.astype(BF16), xb).astype(BF16)

    _run_copies(i, slot, chunk_ref, nchunk_ref, xs_ref, o_ref, sems, True)

    @pl.when(i > 0)
    def _():
        _run_waits(i - 1, 1 - slot, nchunk_ref, xs_ref, o_ref, sems, True)

    @pl.when(i == pl.num_programs(0) - 1)
    def _():
        _run_waits(i, slot, nchunk_ref, xs_ref, o_ref, sems, True)
        trailing(lambda cp: cp.wait())


def _chunk_spec(index):
    return pl.BlockSpec((1, 1, 2 * _N_CHUNK * N_EXPERTS), lambda i, *_: (index(i), 0, 0), memory_space=pltpu.SMEM)


def _dispatch(tail_rows, nused, nchunk, chunks, rt, xn, n_rows):
    T = xn.shape[0]
    tb = MOE_TILE
    return pl.pallas_call(
        _dispatch_kernel,
        grid_spec=pltpu.PrefetchScalarGridSpec(
            num_scalar_prefetch=3,
            grid=(T // tb,),
            in_specs=[_chunk_spec(lambda i: i), pl.BlockSpec((tb, LANES), lambda i, *_: (i, 0)),
                      pl.BlockSpec((tb, D_MODEL), lambda i, *_: (i, 0))],
            out_specs=pl.BlockSpec(memory_space=pl.ANY),
            scratch_shapes=[pltpu.VMEM((2, _TILE_ROWS, D_MODEL), BF16), pltpu.VMEM((MOE_BLK, D_MODEL), BF16),
                            pltpu.SemaphoreType.DMA((2, _N_CHUNK)), pltpu.SemaphoreType.DMA(()),
                            pltpu.SemaphoreType.DMA(())],
        ),
        out_shape=jax.ShapeDtypeStruct((n_rows, D_MODEL), BF16),
        compiler_params=_cp(("arbitrary",)),
        name="moe_dispatch",
    )(tail_rows, nused, nchunk, chunks, rt, xn)


def _expert_kernel(be_ref, nused_ref, first_ref, next_ref, wslot_ref, x_ref, wg_hbm, bg_ref, wu_hbm, bu_ref,
                   wd_hbm, bd_ref, y_ref, stage_ref, wb_ref, sems):
    i = pl.program_id(0)

    def fetch(e, slot):
        return [pltpu.make_async_copy(w.at[e], stage_ref.at[slot, m], sems.at[slot])
                for m, w in enumerate((wg_hbm, wu_hbm, wd_hbm))]

    @pl.when(first_ref[i] != 0)
    def _():
        slot = wslot_ref[i]

        @pl.when(i == 0)
        def _():
            for cp in fetch(be_ref[0], slot):
                cp.start()

        for cp in fetch(be_ref[i], slot):
            cp.wait()
        for m in range(3):
            wb_ref[m] = stage_ref[slot, m].astype(BF16)

        @pl.when(next_ref[i] >= 0)
        def _():
            for cp in fetch(next_ref[i], 1 - slot):
                cp.start()

    @pl.when(i < nused_ref[0])
    def _():
        xb = x_ref[...]
        g = _dot(xb, wb_ref[0]) + bg_ref[0]
        u = _dot(xb, wb_ref[1]) + bu_ref[0]
        g = jnp.minimum(g, SWIGLU_LIMIT)
        u = jnp.clip(u, -SWIGLU_LIMIT, SWIGLU_LIMIT)
        glu = g * _sigmoid(SWIGLU_ALPHA * g)
        y_ref[...] = (_dot(((u + 1.0) * glu).astype(BF16), wb_ref[2]) + bd_ref[0]).astype(BF16)

    @pl.when(i >= nused_ref[0])
    def _():
        y_ref[...] = jnp.zeros_like(y_ref)


def _experts(block_expert, nused, first, next_expert, wslot, x_pad, wg, bg, wu, bu, wd, bd):
    n_rows = x_pad.shape[0]
    wspec = pl.BlockSpec(memory_space=pl.ANY)
    bspec = pl.BlockSpec((1, 1, D_MODEL), lambda i, be, *_: (be[i], 0, 0))
    yspec = pl.BlockSpec((MOE_BLK, D_MODEL), lambda i, *_: (i, 0))
    xspec = pl.BlockSpec((MOE_BLK, D_MODEL), lambda i, be, nu, *_: (jnp.minimum(i, nu[0] - 1), 0))
    return pl.pallas_call(
        _expert_kernel,
        grid_spec=pltpu.PrefetchScalarGridSpec(
            num_scalar_prefetch=5,
            grid=(n_rows // MOE_BLK,),
            in_specs=[xspec, wspec, bspec, wspec, bspec, wspec, bspec],
            out_specs=yspec,
            scratch_shapes=[pltpu.VMEM((2, 3, D_MODEL, D_MODEL), F32), pltpu.VMEM((3, D_MODEL, D_MODEL), BF16),
                            pltpu.SemaphoreType.DMA((2,))],
        ),
        out_shape=jax.ShapeDtypeStruct((n_rows, D_MODEL), BF16),
        compiler_params=_cp(("arbitrary",)),
        name="moe_experts",
    )(block_expert, nused, first, next_expert, wslot, x_pad, wg, bg, wu, bu, wd, bd)


def _tail_kernel(nchunk_ref, chunk_ref, chunk_next_ref, y_ref, rt_ref, h_ref, p_ref, npl_ref,
                 wpg_ref, wpp_ref, nfin_ref, o_ref, ys_ref, sems):
    i = pl.program_id(0)
    slot = i % 2
    fetch = functools.partial(_run_copies, nchunk_ref=nchunk_ref, tile_buf=ys_ref, hbm_ref=y_ref, sems=sems,
                              to_hbm=False)

    @pl.when(i == 0)
    def _():
        ys_ref[...] = jnp.zeros_like(ys_ref)
        fetch(i, slot, chunk_ref)

    @pl.when(i + 1 < pl.num_programs(0))
    def _():
        fetch(i + 1, 1 - slot, chunk_next_ref)

    _run_waits(i, slot, nchunk_ref, ys_ref, y_ref, sems, False)

    rt = rt_ref[...]
    cols = lax.broadcasted_iota(jnp.int32, (MOE_TILE, _TILE_ROWS), 1).astype(F32)
    weight = jnp.zeros((MOE_TILE, _TILE_ROWS), F32)
    for k in range(TOP_K):
        weight = jnp.where(cols == rt[:, TOP_K + k:TOP_K + k + 1],
                           rt[:, 2 * TOP_K + k:2 * TOP_K + k + 1], weight)
    h = h_ref[...] + _dot(weight.astype(BF16), ys_ref[slot])
    pg = _sigmoid(_dot(_rms(h, npl_ref[...]).astype(BF16), wpg_ref[...]))
    h = h + pg * _dot(p_ref[...].astype(BF16), wpp_ref[...])
    o_ref[...] = _rms(h, nfin_ref[...])


def _tail(nchunk, chunks, y_pad, rt, h1, p2, npl, wpg, wpp, nfin):
    T = h1.shape[0]
    tc = MOE_TILE
    last = T // tc - 1
    row = lambda n: pl.BlockSpec((tc, n), lambda i, *_: (i, 0))
    full = lambda a: pl.BlockSpec(a.shape, lambda i, *_: (0,) * a.ndim)
    return pl.pallas_call(
        _tail_kernel,
        grid_spec=pltpu.PrefetchScalarGridSpec(
            num_scalar_prefetch=1,
            grid=(T // tc,),
            in_specs=[_chunk_spec(lambda i: i), _chunk_spec(lambda i: jnp.minimum(i + 1, last)),
                      pl.BlockSpec(memory_space=pl.ANY), row(LANES), row(D_MODEL), row(PLE_DIM),
                      full(npl), full(wpg), full(wpp), full(nfin)],
            out_specs=row(D_MODEL),
            scratch_shapes=[pltpu.VMEM((2, _TILE_ROWS, D_MODEL), BF16), pltpu.SemaphoreType.DMA((2, _N_CHUNK))],
        ),
        out_shape=jax.ShapeDtypeStruct((T, D_MODEL), F32),
        compiler_params=_cp(("arbitrary",)),
        name="moe_combine_tail",
    )(nchunk, chunks, chunks, y_pad, rt, h1, p2, npl, wpg, wpp, nfin)


def _layer(h, p_l, norm_mix, w_in, b_forget, w_attn_out, conv_w, conv_b, conv_ln_g, conv_ln_b, w_conv_out, w_o,
           norm_ffn, w_router, b_router, w_gate, b_gate, w_up, b_up, w_down, b_down, norm_ple, w_ple_gate,
           w_ple_proj):
    B, S, D = h.shape
    T = B * S
    row2 = lambda a: a.reshape(1, -1)

    o_f = 3 * ATTN_WIDTH
    o_conv = o_f + N_HEADS
    reps = LANES // N_HEADS
    w_f = jnp.tile(w_in[:, o_f:o_conv], (1, reps))
    w_all = jnp.concatenate([w_in[:, :ATTN_WIDTH] * (HEAD_DIM ** -0.5), w_in[:, ATTN_WIDTH:o_f],
                             w_in[:, o_conv:], w_f], axis=1).astype(BF16)
    bf = jnp.tile(b_forget, reps).reshape(1, LANES)

    cw = jnp.concatenate([conv_w, jnp.zeros((CONV_HALO - CONV_K, CONV_WIDTH), F32)], axis=0)
    qa, ka, va, act, sga, sgc = _inproj(h, row2(norm_mix), w_all, bf, *_bias_routing(), cw, row2(conv_b),
                                        row2(conv_ln_g), row2(conv_ln_b))
    attn = _attention(qa, ka, va)

    wr = jnp.concatenate([w_router, jnp.zeros((D, LANES - N_EXPERTS), F32)], axis=1).astype(BF16)
    br = jnp.concatenate([b_router, jnp.full((LANES - N_EXPERTS,), NEG_INF, F32)]).reshape(1, LANES)
    h1, xn2, rt, tile_cnt, tile_before, counts = _mix(
        act, attn, sga, sgc, h, w_attn_out.astype(BF16), w_conv_out.astype(BF16),
        w_o.astype(BF16), row2(norm_ffn), wr, br)
    rt = rt.reshape(T, LANES)

    cnt = counts[0, :N_EXPERTS].astype(jnp.int32)
    padded = ((cnt + MOE_BLK - 1) // MOE_BLK) * MOE_BLK
    pad_end = jnp.cumsum(padded)
    pad_start = pad_end - padded
    max_rows = T * TOP_K + (T // MOE_TILE) * N_EXPERTS * (RUN_ALIGN - 1)
    n_blocks = -(-max_rows // MOE_BLK) + N_EXPERTS
    n_rows = n_blocks * MOE_BLK
    n_used = pad_end[-1] // MOE_BLK
    run_len = tile_cnt[:, 0, :N_EXPERTS].astype(jnp.int32)
    run_off = jnp.cumsum(run_len, axis=1) - run_len
    run_dst = pad_start[None, :] + tile_before[:, 0, :N_EXPERTS].astype(jnp.int32)
    bits = jnp.asarray(_CHUNK_BITS, jnp.int32)
    has = (run_len[:, :, None] >> bits) & 1
    done = (run_len[:, :, None] >> (bits + 1)) << (bits + 1)
    slot_in_list = jnp.cumsum(has, axis=1) - 1
    hit = (has[..., None] == 1) & (slot_in_list[..., None] == jnp.arange(N_EXPERTS, dtype=jnp.int32))
    pick = lambda v: jnp.sum(jnp.where(hit, v[..., None], 0), axis=1)
    chunks = jnp.stack([pick(run_off[:, :, None] + done), pick(run_dst[:, :, None] + done)], axis=2)
    chunks = chunks.reshape(-1, 1, 2 * _N_CHUNK * N_EXPERTS).astype(jnp.int32)
    nchunk = jnp.sum(has, axis=1).astype(jnp.int32).reshape(-1)
    blk = jnp.arange(n_blocks, dtype=jnp.int32)
    block_expert = jnp.minimum(
        jnp.sum((pad_end[None, :] <= (blk * MOE_BLK)[:, None]).astype(jnp.int32), axis=1), N_EXPERTS - 1)
    fill_rows = jnp.where(cnt > 0, pad_end - MOE_BLK, n_used * MOE_BLK).astype(jnp.int32)

    nused = n_used.astype(jnp.int32).reshape(1)
    first = (blk < n_used) & ((blk == 0) | (block_expert != jnp.roll(block_expert, 1)))
    wslot = (jnp.cumsum(first.astype(jnp.int32)) - 1) % 2
    later_first = jnp.where(first[None, :] & (blk[None, :] > blk[:, None]), blk[None, :], n_blocks)
    next_pos = jnp.min(later_first, axis=1)
    next_expert = jnp.sum(jnp.where(blk[None, :] == next_pos[:, None], block_expert[None, :], 0), axis=1)
    next_expert = jnp.where(next_pos < n_blocks, next_expert, -1).astype(jnp.int32)

    x_pad = _dispatch(fill_rows, nused, nchunk, chunks, rt, xn2.reshape(T, D), n_rows)
    y_pad = _experts(block_expert, nused, first.astype(jnp.int32), next_expert, wslot.astype(jnp.int32), x_pad,
                     w_gate, b_gate.reshape(N_EXPERTS, 1, D), w_up,
                     b_up.reshape(N_EXPERTS, 1, D), w_down, b_down.reshape(N_EXPERTS, 1, D))
    return _tail, (nchunk, chunks, y_pad, rt, h1.reshape(T, D),
                   p_l.reshape(T, PLE_DIM), row2(norm_ple), w_ple_gate.astype(BF16), w_ple_proj.astype(BF16))


def kernel(x, p, norm_mix, w_in, b_forget, w_attn_out, conv_w, conv_b, conv_ln_g, conv_ln_b, w_conv_out, w_o,
           norm_ffn, w_router, b_router, w_gate, b_gate, w_up, b_up, w_down, b_down, norm_ple, w_ple_gate,
           w_ple_proj, norm_final):
    B, S, D = x.shape
    assert p.shape[0] == 1 and D == D_MODEL
    tail, args = _layer(x, p[0], norm_mix[0], w_in[0], b_forget[0], w_attn_out[0], conv_w[0], conv_b[0],
                        conv_ln_g[0], conv_ln_b[0], w_conv_out[0], w_o[0], norm_ffn[0], w_router[0],
                        b_router[0], w_gate[0], b_gate[0], w_up[0], b_up[0], w_down[0], b_down[0],
                        norm_ple[0], w_ple_gate[0], w_ple_proj[0])
    return tail(*args, norm_final.reshape(1, D)).reshape(B, S, D)
```

```python
import functools

import jax
import jax.numpy as jnp
import numpy as np
from jax import lax
from jax.experimental import pallas as pl
from jax.experimental.pallas import tpu as pltpu

F32 = jnp.float32
BF16 = jnp.bfloat16

D_MODEL = 1024
HEAD_DIM = 64
N_HEADS = 8
ATTN_WIDTH = N_HEADS * HEAD_DIM
CONV_WIDTH = 512
CONV_K = 31
N_EXPERTS = 32
TOP_K = 4
SWIGLU_LIMIT = 7.0
SWIGLU_ALPHA = 1.702
PLE_DIM = 256
RMS_EPS = 1e-6
LN_EPS = 1e-5
NEG_INF = -1e30

LANES = 128
SUBLANES = 8
VMEM_LIMIT = 56 * 1024 * 1024

ROW_TILE = 512
ATTN_T = 512
HEAD_PAD = 128
CONV_HALO = 32
MOE_BLK = 512
MOE_TILE = 512
RUN_ALIGN = 16


def _cp(sem):
    return pltpu.CompilerParams(dimension_semantics=sem, vmem_limit_bytes=VMEM_LIMIT)


def _dot(a, b):
    return jnp.dot(a, b, preferred_element_type=F32)


def _sigmoid(x):
    return 0.5 * jnp.tanh(0.5 * x) + 0.5


def _rms(x, g):
    return x * lax.rsqrt(jnp.mean(x * x, axis=-1, keepdims=True) + RMS_EPS) * g


_C_QKV = 3 * ATTN_WIDTH
_C_CONV = _C_QKV + 2 * CONV_WIDTH
_C_GA = _C_CONV + D_MODEL
_C_GC = _C_GA + D_MODEL
_C_END = _C_GC + LANES


_HP_ALL = N_HEADS * HEAD_PAD
_N_CSPLIT = 3


def _split3(x):
    hi = x.astype(BF16)
    r1 = x - hi.astype(F32)
    mid = r1.astype(BF16)
    lo = (r1 - mid.astype(F32)).astype(BF16)
    return hi, mid, lo


def _store_heads(o_ref, src, aux):
    low = lax.broadcasted_iota(jnp.int32, (src.shape[0], LANES), 1) < HEAD_DIM
    for h in range(N_HEADS):
        chunk = src[:, (h // 2) * LANES:(h // 2 + 1) * LANES]
        if h % 2:
            chunk = pltpu.roll(chunk, HEAD_DIM, 1)
        o_ref[0, :, h * HEAD_PAD:(h + 1) * HEAD_PAD] = jnp.where(
            low, chunk, aux[:, h * HEAD_PAD:(h + 1) * HEAD_PAD]).astype(BF16)


def _conv_act(z, zs_ref, ph_ref, cv_ref, cw_ref, cb_ref, lng_ref, lnb_ref):
    ts = z.shape[0]
    zs_ref[0:CONV_HALO, :] = zs_ref[ts:ts + CONV_HALO, :]
    zs_ref[CONV_HALO:CONV_HALO + ts, :] = z
    base = CONV_HALO - (CONV_K - 1)
    for ph in range(SUBLANES):
        rows = ts + SUBLANES * ((CONV_K - 1 - ph) // SUBLANES)
        ph_ref[ph, 0:rows, :] = zs_ref[base + ph:base + ph + rows, :]
    rc = 128
    for c0 in range(0, CONV_WIDTH, LANES):
        for r0 in range(0, ts, rc):
            acc = jnp.zeros((rc, LANES), F32)
            for j in range(CONV_K):
                a0 = r0 + SUBLANES * (j // SUBLANES)
                acc = acc + cw_ref[j:j + 1, c0:c0 + LANES] * ph_ref[j % SUBLANES, a0:a0 + rc, c0:c0 + LANES]
            cv_ref[r0:r0 + rc, c0:c0 + LANES] = acc
    cv = cv_ref[...] + cb_ref[...]
    mu = jnp.mean(cv, axis=-1, keepdims=True)
    d = cv - mu
    var = jnp.mean(d * d, axis=-1, keepdims=True)
    y = d * lax.rsqrt(var + LN_EPS) * lng_ref[...] + lnb_ref[...]
    return (y * _sigmoid(y)).astype(BF16)


def _inproj_kernel(x_ref, g_ref, w_ref, bf_ref, selq_ref, selk_ref, auxq_ref, auxk_ref, auxv_ref,
                   cw_ref, cb_ref, lng_ref, lnb_ref, qa_ref, ka_ref, va_ref, act_ref, sga_ref, sgc_ref,
                   carry_ref, zs_ref, ph_ref, cv_ref):
    tm = x_ref.shape[1]

    @pl.when(pl.program_id(1) == 0)
    def _():
        carry_ref[...] = jnp.zeros_like(carry_ref)
        zs_ref[tm:tm + CONV_HALO, :] = jnp.zeros((CONV_HALO, CONV_WIDTH), F32)

    xn = _rms(x_ref[0], g_ref[...]).astype(BF16)

    f = _dot(xn, w_ref[:, _C_GC:_C_END]) + bf_ref[...]
    lf = jnp.minimum(f, 0.0) - jnp.log(1.0 + jnp.exp(-jnp.abs(f)))
    r = lax.broadcasted_iota(jnp.int32, (tm, tm), 0)
    c = lax.broadcasted_iota(jnp.int32, (tm, tm), 1)
    tri = jnp.where(c <= r, 1.0, 0.0).astype(BF16)
    hi, mid, lo = _split3(lf)
    cs = _dot(tri, hi) + _dot(tri, mid) + _dot(tri, lo) + carry_ref[...]
    carry_ref[...] = cs[tm - 1:tm, :]
    chi, cmid, clo = (a.astype(F32) for a in _split3(cs))
    group = lax.broadcasted_iota(jnp.int32, (tm, LANES), 1) // N_HEADS
    csel = jnp.where(group == 0, chi, jnp.where(group == 1, cmid, jnp.where(
        group == 2, clo, jnp.where(group == 3, -chi, jnp.where(group == 4, -cmid, -clo))))).astype(BF16)
    aux_q = _dot(csel, selq_ref[...]) + auxq_ref[...]
    aux_k = _dot(csel, selk_ref[...]) + auxk_ref[...]

    qkv = _dot(xn, w_ref[:, 0:_C_QKV])
    _store_heads(qa_ref, qkv[:, 0:ATTN_WIDTH], aux_q)
    _store_heads(ka_ref, qkv[:, ATTN_WIDTH:2 * ATTN_WIDTH], aux_k)
    _store_heads(va_ref, qkv[:, 2 * ATTN_WIDTH:3 * ATTN_WIDTH], auxv_ref[...])
    u = _dot(xn, w_ref[:, _C_QKV:_C_CONV])
    z = u[:, 0:CONV_WIDTH] * _sigmoid(u[:, CONV_WIDTH:])
    act_ref[0] = _conv_act(z, zs_ref, ph_ref, cv_ref, cw_ref, cb_ref, lng_ref, lnb_ref)
    sga_ref[0] = _sigmoid(_dot(xn, w_ref[:, _C_CONV:_C_GA])).astype(BF16)
    sgc_ref[0] = _sigmoid(_dot(xn, w_ref[:, _C_GA:_C_GC])).astype(BF16)


def _inproj(x, g, w, bf, selq, selk, auxq, auxk, auxv, cw, cb, lng, lnb):
    B, S, _ = x.shape
    tm = ROW_TILE
    row = lambda n: pl.BlockSpec((1, tm, n), lambda b, i: (b, i, 0))
    full = lambda a: pl.BlockSpec(a.shape, lambda b, i: (0,) * a.ndim, pipeline_mode=pl.Buffered(1))
    sds = lambda n, dt: jax.ShapeDtypeStruct((B, S, n), dt)
    consts = (g, w, bf, selq, selk, auxq, auxk, auxv, cw, cb, lng, lnb)
    return pl.pallas_call(
        _inproj_kernel,
        grid=(B, S // tm),
        in_specs=[row(D_MODEL)] + [full(a) for a in consts],
        out_specs=[row(_HP_ALL), row(_HP_ALL), row(_HP_ALL), row(CONV_WIDTH), row(D_MODEL), row(D_MODEL)],
        out_shape=[sds(_HP_ALL, BF16)] * 3 + [sds(CONV_WIDTH, BF16), sds(D_MODEL, BF16), sds(D_MODEL, BF16)],
        scratch_shapes=[pltpu.VMEM((1, LANES), F32), pltpu.VMEM((CONV_HALO + tm, CONV_WIDTH), F32),
                        pltpu.VMEM((SUBLANES, tm + CONV_HALO - SUBLANES, CONV_WIDTH), F32),
                        pltpu.VMEM((tm, CONV_WIDTH), F32)],
        compiler_params=_cp(("parallel", "arbitrary")),
        name="inproj",
    )(x, g, w, bf, selq, selk, auxq, auxk, auxv, cw, cb, lng, lnb)


def _bias_routing():
    selq = np.zeros((LANES, _HP_ALL), np.float32)
    selk = np.zeros((LANES, _HP_ALL), np.float32)
    auxq = np.zeros((1, _HP_ALL), np.float32)
    auxk = np.zeros((1, _HP_ALL), np.float32)
    auxv = np.zeros((1, _HP_ALL), np.float32)
    for h in range(N_HEADS):
        base = h * HEAD_PAD + HEAD_DIM
        for g in range(_N_CSPLIT):
            selq[g * N_HEADS + h, base + g] = 1.0
            auxk[0, base + g] = 1.0
            auxq[0, base + _N_CSPLIT + g] = 1.0
            selk[(_N_CSPLIT + g) * N_HEADS + h, base + _N_CSPLIT + g] = 1.0
        auxv[0, base] = 1.0
    return (jnp.asarray(selq, BF16), jnp.asarray(selk, BF16), jnp.asarray(auxq), jnp.asarray(auxk),
            jnp.asarray(auxv))


_HEADS_PER_STEP = 4


def _attn_kernel(qa_ref, ka_ref, va_ref, o_ref, s_ref):
    i = pl.program_id(2)
    t = ATTN_T
    dn_last = (((1,), (1,)), ((), ()))
    dn_first = (((0,), (0,)), ((), ()))

    def scores(j, slot):
        r0 = pl.multiple_of(j * t, t)
        for hh in range(_HEADS_PER_STEP):
            lanes = slice(hh * HEAD_PAD, (hh + 1) * HEAD_PAD)
            s_ref[slot, hh] = lax.dot_general(ka_ref[0, pl.ds(r0, t), lanes], qa_ref[0, :, lanes], dn_last,
                                              preferred_element_type=F32)

    def consume(j, slot, carry, masked):
        r0 = pl.multiple_of(j * t, t)
        new = []
        for hh in range(_HEADS_PER_STEP):
            m, acc = carry[hh]
            lanes = slice(hh * HEAD_PAD, (hh + 1) * HEAD_PAD)
            s = s_ref[slot, hh]
            if masked:
                key = lax.broadcasted_iota(jnp.int32, (t, t), 0)
                qry = lax.broadcasted_iota(jnp.int32, (t, t), 1)
                s = jnp.where(key <= qry, s, NEG_INF)
            m_new = jnp.maximum(m, jnp.max(s, axis=0, keepdims=True))
            p = jnp.exp(s - m_new).astype(BF16)
            pv = lax.dot_general(va_ref[0, pl.ds(r0, t), lanes], p, dn_first, preferred_element_type=F32)
            new.append((m_new, jnp.exp(m - m_new) * acc + pv))
        return tuple(new)

    def finish(carry):
        outs = [(acc[0:HEAD_DIM, :] / acc[HEAD_DIM:HEAD_DIM + 1, :]).T for _, acc in carry]
        o_ref[0] = jnp.concatenate(outs, axis=-1).astype(BF16)

    def pair(pp, carry):
        j = 2 * pp
        scores(j + 1, 1)
        carry = consume(j, 0, carry, False)
        scores(j + 2, 0)
        return consume(j + 1, 1, carry, False)

    init = tuple((jnp.full((1, t), NEG_INF, F32), jnp.zeros((HEAD_PAD, t), F32))
                 for _ in range(_HEADS_PER_STEP))
    scores(0, 0)
    carry = lax.fori_loop(0, i // 2, pair, init)

    @pl.when(i % 2 == 0)
    def _():
        finish(consume(i, 0, carry, True))

    @pl.when(i % 2 == 1)
    def _():
        scores(i, 1)
        finish(consume(i, 1, consume(i - 1, 0, carry, False), True))


def _attention(qa, ka, va):
    B, S, _ = qa.shape
    t = ATTN_T
    w = _HEADS_PER_STEP * HEAD_PAD
    return pl.pallas_call(
        _attn_kernel,
        grid=(B, N_HEADS // _HEADS_PER_STEP, S // t),
        in_specs=[pl.BlockSpec((1, t, w), lambda b, h, i: (b, i, h)),
                  pl.BlockSpec((1, S, w), lambda b, h, i: (b, 0, h)),
                  pl.BlockSpec((1, S, w), lambda b, h, i: (b, 0, h))],
        out_specs=pl.BlockSpec((1, t, _HEADS_PER_STEP * HEAD_DIM), lambda b, h, i: (b, i, h)),
        out_shape=jax.ShapeDtypeStruct((B, S, ATTN_WIDTH), BF16),
        scratch_shapes=[pltpu.VMEM((2, _HEADS_PER_STEP, t, t), F32)],
        compiler_params=_cp(("parallel", "parallel", "arbitrary")),
        name="fox_attention",
    )(qa, ka, va)


def _mix_kernel(act_ref, attn_ref, sga_ref, sgc_ref, x_ref, wa_ref, wc_ref, wo_ref, nf_ref, wr_ref, br_ref,
                h_ref, xn_ref, rt_ref, tcnt_ref, tbef_ref, cnt_ref, carry_ref):
    i = pl.program_id(1)
    branch_c = _dot(act_ref[0], wc_ref[...])
    branch_a = _dot(attn_ref[0], wa_ref[...])
    merged = sga_ref[0].astype(F32) * branch_a + sgc_ref[0].astype(F32) * branch_c
    h = x_ref[0] + _dot(merged.astype(BF16), wo_ref[...])
    h_ref[0] = h
    xn = _rms(h, nf_ref[...]).astype(BF16)
    xn_ref[0] = xn

    @pl.when((pl.program_id(0) == 0) & (i == 0))
    def _():
        carry_ref[...] = jnp.zeros_like(carry_ref)

    rt, n_tile = _route_tile(_dot(xn, wr_ref[...]) + br_ref[...])
    rt_ref[0] = rt
    tcnt_ref[0] = n_tile
    tbef_ref[0] = carry_ref[...]
    total = carry_ref[...] + n_tile
    carry_ref[...] = total
    cnt_ref[...] = total


def _mix(act, attn, sga, sgc, x, wa, wc, wo, nf, wr, br):
    B, S, _ = x.shape
    ts = ROW_TILE
    assert ts == MOE_TILE, "routing runs on the mix kernel's row tiles"
    nt = S // ts
    row = lambda n: pl.BlockSpec((1, ts, n), lambda b, i: (b, i, 0))
    full = lambda a: pl.BlockSpec(a.shape, lambda b, i: (0,) * a.ndim)
    per_tile = pl.BlockSpec((1, 1, LANES), lambda b, i: (b * nt + i, 0, 0))
    return pl.pallas_call(
        _mix_kernel,
        grid=(B, nt),
        in_specs=[row(CONV_WIDTH), row(ATTN_WIDTH), row(D_MODEL), row(D_MODEL), row(D_MODEL),
                  full(wa), full(wc), full(wo), full(nf), full(wr), full(br)],
        out_specs=[row(D_MODEL), row(D_MODEL), row(LANES), per_tile, per_tile,
                   pl.BlockSpec((1, LANES), lambda b, i: (0, 0))],
        out_shape=[jax.ShapeDtypeStruct((B, S, D_MODEL), F32), jax.ShapeDtypeStruct((B, S, D_MODEL), BF16),
                   jax.ShapeDtypeStruct((B, S, LANES), F32), jax.ShapeDtypeStruct((B * nt, 1, LANES), F32),
                   jax.ShapeDtypeStruct((B * nt, 1, LANES), F32), jax.ShapeDtypeStruct((1, LANES), F32)],
        scratch_shapes=[pltpu.VMEM((1, LANES), F32)],
        compiler_params=_cp(("arbitrary", "arbitrary")),
        name="mix_merge",
    )(act, attn, sga, sgc, x, wa, wc, wo, nf, wr, br)


def _route_tile(vals):
    tr = vals.shape[0]
    lane = lax.broadcasted_iota(jnp.int32, (tr, LANES), 1).astype(F32)
    top_v, top_i, hots = [], [], []
    for _ in range(TOP_K):
        m = jnp.max(vals, axis=-1, keepdims=True)
        idx = jnp.min(jnp.where(vals == m, lane, float(LANES)), axis=-1, keepdims=True)
        hot = lane == idx
        vals = jnp.where(hot, -jnp.inf, vals)
        top_v.append(m)
        top_i.append(idx)
        hots.append(hot)
    ex = [jnp.exp(v - top_v[0]) for v in top_v]
    den = ex[0] + ex[1] + ex[2] + ex[3]
    onehot = jnp.zeros((tr, LANES), F32)
    for hot in hots:
        onehot = onehot + jnp.where(hot, 1.0, 0.0)
    r = lax.broadcasted_iota(jnp.int32, (tr, tr), 0)
    c = lax.broadcasted_iota(jnp.int32, (tr, tr), 1)
    strict = jnp.where(c < r, 1.0, 0.0).astype(BF16)
    earlier = _dot(strict, onehot.astype(BF16))
    n_tile = jnp.sum(onehot, axis=0, keepdims=True)
    n_tile = jnp.floor((n_tile + (RUN_ALIGN - 1)) * (1.0 / RUN_ALIGN)) * RUN_ALIGN
    li = lax.broadcasted_iota(jnp.int32, (LANES, LANES), 0)
    lj = lax.broadcasted_iota(jnp.int32, (LANES, LANES), 1)
    lower = jnp.where(li < lj, 1.0, 0.0).astype(BF16)
    run_off = _dot(jnp.broadcast_to(n_tile, (8, LANES)).astype(BF16), lower)[0:1, :]
    within = earlier + run_off
    out = jnp.zeros((tr, LANES), F32)
    for k in range(TOP_K):
        pos = jnp.sum(jnp.where(hots[k], within, 0.0), axis=-1, keepdims=True)
        out = jnp.where(lane == k, top_i[k], out)
        out = jnp.where(lane == TOP_K + k, pos, out)
        out = jnp.where(lane == 2 * TOP_K + k, ex[k] / den, out)
    return out, n_tile


_ALIGN_BITS = RUN_ALIGN.bit_length() - 1
_TILE_ROWS = MOE_TILE * TOP_K + N_EXPERTS * RUN_ALIGN
_CHUNK_BITS = tuple(range(_ALIGN_BITS, MOE_TILE.bit_length()))
_N_CHUNK = len(_CHUNK_BITS)


def _run_copies(tile, slot, chunk_ref, nchunk_ref, tile_buf, hbm_ref, sems, to_hbm):
    for s, b in enumerate(_CHUNK_BITS):
        size = 1 << b

        def one_chunk(k, c):
            off = chunk_ref[0, 0, (2 * s) * N_EXPERTS + k]
            dst = chunk_ref[0, 0, (2 * s + 1) * N_EXPERTS + k]
            loc = tile_buf.at[slot, pl.ds(pl.multiple_of(off, RUN_ALIGN), size)]
            far = hbm_ref.at[pl.ds(pl.multiple_of(dst, RUN_ALIGN), size)]
            src, dstn = (loc, far) if to_hbm else (far, loc)
            pltpu.make_async_copy(src, dstn, sems.at[slot, s]).start(priority=s % 2)
            return c

        lax.fori_loop(0, nchunk_ref[tile * _N_CHUNK + s], one_chunk, 0)


def _run_waits(tile, slot, nchunk_ref, tile_buf, hbm_ref, sems, to_hbm):
    for s, b in enumerate(_CHUNK_BITS):
        size = 1 << b
        loc = tile_buf.at[slot, pl.ds(0, size)]
        far = hbm_ref.at[pl.ds(0, size)]
        src, dstn = (loc, far) if to_hbm else (far, loc)
        cp = pltpu.make_async_copy(src, dstn, sems.at[slot, s])
        lax.fori_loop(0, nchunk_ref[tile * _N_CHUNK + s], lambda n, c: (cp.wait(), c)[1], 0)


def _dispatch_kernel(tail_ref, nused_ref, nchunk_ref, chunk_ref,
                     rt_ref, x_ref, o_ref, xs_ref, zero_ref, sems, fill_sem, trail_sem):
    i = pl.program_id(0)
    slot = i % 2
    n_blocks = o_ref.shape[0] // MOE_BLK

    def fill(row, sem):
        return pltpu.make_async_copy(zero_ref, o_ref.at[pl.ds(pl.multiple_of(row, MOE_BLK), MOE_BLK)], sem)

    def trailing(op):
        lax.fori_loop(nused_ref[0], n_blocks, lambda n, c: (op(fill(n * MOE_BLK, trail_sem)), c)[1], 0)

    @pl.when(i == 0)
    def _():
        zero_ref[...] = jnp.zeros_like(zero_ref)
        lax.fori_loop(0, N_EXPERTS, lambda n, c: (fill(tail_ref[n], fill_sem).start(), c)[1], 0)
        trailing(lambda cp: cp.start())
        lax.fori_loop(0, N_EXPERTS, lambda n, c: (fill(tail_ref[n], fill_sem).wait(), c)[1], 0)

    pos_t = rt_ref[...].T
    xb = x_ref[...]
    for r0 in range(0, _TILE_ROWS, MOE_TILE):
        rows = (lax.broadcasted_iota(jnp.int32, (MOE_TILE, MOE_TILE), 0) + r0).astype(F32)
        place = jnp.zeros((MOE_TILE, MOE_TILE), F32)
        for k in range(TOP_K):
            place = jnp.where(rows == pos_t[TOP_K + k:TOP_K + k + 1, :], 1.0, place)
        xs_ref[slot, r0:r0 + MOE_TILE, :] = _dot(place.astype(BF16), xb).astype(BF16)

    _run_copies(i, slot, chunk_ref, nchunk_ref, xs_ref, o_ref, sems, True)

    @pl.when(i > 0)
    def _():
        _run_waits(i - 1, 1 - slot, nchunk_ref, xs_ref, o_ref, sems, True)

    @pl.when(i == pl.num_programs(0) - 1)
    def _():
        _run_waits(i, slot, nchunk_ref, xs_ref, o_ref, sems, True)
        trailing(lambda cp: cp.wait())


def _chunk_spec(index):
    return pl.BlockSpec((1, 1, 2 * _N_CHUNK * N_EXPERTS), lambda i, *_: (index(i), 0, 0), memory_space=pltpu.SMEM)


def _dispatch(tail_rows, nused, nchunk, chunks, rt, xn, n_rows):
    T = xn.shape[0]
    tb = MOE_TILE
    return pl.pallas_call(
        _dispatch_kernel,
        grid_spec=pltpu.PrefetchScalarGridSpec(
            num_scalar_prefetch=3,
            grid=(T // tb,),
            in_specs=[_chunk_spec(lambda i: i), pl.BlockSpec((tb, LANES), lambda i, *_: (i, 0)),
                      pl.BlockSpec((tb, D_MODEL), lambda i, *_: (i, 0))],
            out_specs=pl.BlockSpec(memory_space=pl.ANY),
            scratch_shapes=[pltpu.VMEM((2, _TILE_ROWS, D_MODEL), BF16), pltpu.VMEM((MOE_BLK, D_MODEL), BF16),
                            pltpu.SemaphoreType.DMA((2, _N_CHUNK)), pltpu.SemaphoreType.DMA(()),
                            pltpu.SemaphoreType.DMA(())],
        ),
        out_shape=jax.ShapeDtypeStruct((n_rows, D_MODEL), BF16),
        compiler_params=_cp(("arbitrary",)),
        name="moe_dispatch",
    )(tail_rows, nused, nchunk, chunks, rt, xn)


def _expert_kernel(be_ref, nused_ref, first_ref, next_ref, wslot_ref, x_ref, wg_hbm, bg_ref, wu_hbm, bu_ref,
                   wd_hbm, bd_ref, y_ref, stage_ref, wb_ref, sems):
    i = pl.program_id(0)

    def fetch(e, slot):
        return [pltpu.make_async_copy(w.at[e], stage_ref.at[slot, m], sems.at[slot])
                for m, w in enumerate((wg_hbm, wu_hbm, wd_hbm))]

    @pl.when(first_ref[i] != 0)
    def _():
        slot = wslot_ref[i]

        @pl.when(i == 0)
        def _():
            for cp in fetch(be_ref[0], slot):
                cp.start()

        for cp in fetch(be_ref[i], slot):
            cp.wait()
        for m in range(3):
            wb_ref[m] = stage_ref[slot, m].astype(BF16)

        @pl.when(next_ref[i] >= 0)
        def _():
            for cp in fetch(next_ref[i], 1 - slot):
                cp.start()

    @pl.when(i < nused_ref[0])
    def _():
        xb = x_ref[...]
        g = _dot(xb, wb_ref[0]) + bg_ref[0]
        u = _dot(xb, wb_ref[1]) + bu_ref[0]
        g = jnp.minimum(g, SWIGLU_LIMIT)
        u = jnp.clip(u, -SWIGLU_LIMIT, SWIGLU_LIMIT)
        glu = g * _sigmoid(SWIGLU_ALPHA * g)
        y_ref[...] = (_dot(((u + 1.0) * glu).astype(BF16), wb_ref[2]) + bd_ref[0]).astype(BF16)

    @pl.when(i >= nused_ref[0])
    def _():
        y_ref[...] = jnp.zeros_like(y_ref)


def _experts(block_expert, nused, first, next_expert, wslot, x_pad, wg, bg, wu, bu, wd, bd):
    n_rows = x_pad.shape[0]
    wspec = pl.BlockSpec(memory_space=pl.ANY)
    bspec = pl.BlockSpec((1, 1, D_MODEL), lambda i, be, *_: (be[i], 0, 0))
    yspec = pl.BlockSpec((MOE_BLK, D_MODEL), lambda i, *_: (i, 0))
    xspec = pl.BlockSpec((MOE_BLK, D_MODEL), lambda i, be, nu, *_: (jnp.minimum(i, nu[0] - 1), 0))
    return pl.pallas_call(
        _expert_kernel,
        grid_spec=pltpu.PrefetchScalarGridSpec(
            num_scalar_prefetch=5,
            grid=(n_rows // MOE_BLK,),
            in_specs=[xspec, wspec, bspec, wspec, bspec, wspec, bspec],
            out_specs=yspec,
            scratch_shapes=[pltpu.VMEM((2, 3, D_MODEL, D_MODEL), F32), pltpu.VMEM((3, D_MODEL, D_MODEL), BF16),
                            pltpu.SemaphoreType.DMA((2,))],
        ),
        out_shape=jax.ShapeDtypeStruct((n_rows, D_MODEL), BF16),
        compiler_params=_cp(("arbitrary",)),
        name="moe_experts",
    )(block_expert, nused, first, next_expert, wslot, x_pad, wg, bg, wu, bu, wd, bd)


def _tail_kernel(nchunk_ref, chunk_ref, chunk_next_ref, y_ref, rt_ref, h_ref, p_ref, npl_ref,
                 wpg_ref, wpp_ref, nfin_ref, o_ref, ys_ref, sems):
    i = pl.program_id(0)
    slot = i % 2
    fetch = functools.partial(_run_copies, nchunk_ref=nchunk_ref, tile_buf=ys_ref, hbm_ref=y_ref, sems=sems,
                              to_hbm=False)

    @pl.when(i == 0)
    def _():
        ys_ref[...] = jnp.zeros_like(ys_ref)
        fetch(i, slot, chunk_ref)

    @pl.when(i + 1 < pl.num_programs(0))
    def _():
        fetch(i + 1, 1 - slot, chunk_next_ref)

    _run_waits(i, slot, nchunk_ref, ys_ref, y_ref, sems, False)

    rt = rt_ref[...]
    cols = lax.broadcasted_iota(jnp.int32, (MOE_TILE, _TILE_ROWS), 1).astype(F32)
    weight = jnp.zeros((MOE_TILE, _TILE_ROWS), F32)
    for k in range(TOP_K):
        weight = jnp.where(cols == rt[:, TOP_K + k:TOP_K + k + 1],
                           rt[:, 2 * TOP_K + k:2 * TOP_K + k + 1], weight)
    h = h_ref[...] + _dot(weight.astype(BF16), ys_ref[slot])
    pg = _sigmoid(_dot(_rms(h, npl_ref[...]).astype(BF16), wpg_ref[...]))
    h = h + pg * _dot(p_ref[...].astype(BF16), wpp_ref[...])
    o_ref[...] = _rms(h, nfin_ref[...])


def _tail(nchunk, chunks, y_pad, rt, h1, p2, npl, wpg, wpp, nfin):
    T = h1.shape[0]
    tc = MOE_TILE
    last = T // tc - 1
    row = lambda n: pl.BlockSpec((tc, n), lambda i, *_: (i, 0))
    full = lambda a: pl.BlockSpec(a.shape, lambda i, *_: (0,) * a.ndim)
    return pl.pallas_call(
        _tail_kernel,
        grid_spec=pltpu.PrefetchScalarGridSpec(
            num_scalar_prefetch=1,
            grid=(T // tc,),
            in_specs=[_chunk_spec(lambda i: i), _chunk_spec(lambda i: jnp.minimum(i + 1, last)),
                      pl.BlockSpec(memory_space=pl.ANY), row(LANES), row(D_MODEL), row(PLE_DIM),
                      full(npl), full(wpg), full(wpp), full(nfin)],
            out_specs=row(D_MODEL),
            scratch_shapes=[pltpu.VMEM((2, _TILE_ROWS, D_MODEL), BF16), pltpu.SemaphoreType.DMA((2, _N_CHUNK))],
        ),
        out_shape=jax.ShapeDtypeStruct((T, D_MODEL), F32),
        compiler_params=_cp(("arbitrary",)),
        name="moe_combine_tail",
    )(nchunk, chunks, chunks, y_pad, rt, h1, p2, npl, wpg, wpp, nfin)


def _layer(h, p_l, norm_mix, w_in, b_forget, w_attn_out, conv_w, conv_b, conv_ln_g, conv_ln_b, w_conv_out, w_o,
           norm_ffn, w_router, b_router, w_gate, b_gate, w_up, b_up, w_down, b_down, norm_ple, w_ple_gate,
           w_ple_proj):
    B, S, D = h.shape
    T = B * S
    row2 = lambda a: a.reshape(1, -1)

    o_f = 3 * ATTN_WIDTH
    o_conv = o_f + N_HEADS
    reps = LANES // N_HEADS
    w_f = jnp.tile(w_in[:, o_f:o_conv], (1, reps))
    w_all = jnp.concatenate([w_in[:, :ATTN_WIDTH] * (HEAD_DIM ** -0.5), w_in[:, ATTN_WIDTH:o_f],
                             w_in[:, o_conv:], w_f], axis=1).astype(BF16)
    bf = jnp.tile(b_forget, reps).reshape(1, LANES)

    cw = jnp.concatenate([conv_w, jnp.zeros((CONV_HALO - CONV_K, CONV_WIDTH), F32)], axis=0)
    qa, ka, va, act, sga, sgc = _inproj(h, row2(norm_mix), w_all, bf, *_bias_routing(), cw, row2(conv_b),
                                        row2(conv_ln_g), row2(conv_ln_b))
    attn = _attention(qa, ka, va)

    wr = jnp.concatenate([w_router, jnp.zeros((D, LANES - N_EXPERTS), F32)], axis=1).astype(BF16)
    br = jnp.concatenate([b_router, jnp.full((LANES - N_EXPERTS,), NEG_INF, F32)]).reshape(1, LANES)
    h1, xn2, rt, tile_cnt, tile_before, counts = _mix(
        act, attn, sga, sgc, h, w_attn_out.astype(BF16), w_conv_out.astype(BF16),
        w_o.astype(BF16), row2(norm_ffn), wr, br)
    rt = rt.reshape(T, LANES)

    cnt = counts[0, :N_EXPERTS].astype(jnp.int32)
    padded = ((cnt + MOE_BLK - 1) // MOE_BLK) * MOE_BLK
    pad_end = jnp.cumsum(padded)
    pad_start = pad_end - padded
    max_rows = T * TOP_K + (T // MOE_TILE) * N_EXPERTS * (RUN_ALIGN - 1)
    n_blocks = -(-max_rows // MOE_BLK) + N_EXPERTS
    n_rows = n_blocks * MOE_BLK
    n_used = pad_end[-1] // MOE_BLK
    run_len = tile_cnt[:, 0, :N_EXPERTS].astype(jnp.int32)
    run_off = jnp.cumsum(run_len, axis=1) - run_len
    run_dst = pad_start[None, :] + tile_before[:, 0, :N_EXPERTS].astype(jnp.int32)
    bits = jnp.asarray(_CHUNK_BITS, jnp.int32)
    has = (run_len[:, :, None] >> bits) & 1
    done = (run_len[:, :, None] >> (bits + 1)) << (bits + 1)
    slot_in_list = jnp.cumsum(has, axis=1) - 1
    hit = (has[..., None] == 1) & (slot_in_list[..., None] == jnp.arange(N_EXPERTS, dtype=jnp.int32))
    pick = lambda v: jnp.sum(jnp.where(hit, v[..., None], 0), axis=1)
    chunks = jnp.stack([pick(run_off[:, :, None] + done), pick(run_dst[:, :, None] + done)], axis=2)
    chunks = chunks.reshape(-1, 1, 2 * _N_CHUNK * N_EXPERTS).astype(jnp.int32)
    nchunk = jnp.sum(has, axis=1).astype(jnp.int32).reshape(-1)
    blk = jnp.arange(n_blocks, dtype=jnp.int32)
    block_expert = jnp.minimum(
        jnp.sum((pad_end[None, :] <= (blk * MOE_BLK)[:, None]).astype(jnp.int32), axis=1), N_EXPERTS - 1)
    fill_rows = jnp.where(cnt > 0, pad_end - MOE_BLK, n_used * MOE_BLK).astype(jnp.int32)

    nused = n_used.astype(jnp.int32).reshape(1)
    first = (blk < n_used) & ((blk == 0) | (block_expert != jnp.roll(block_expert, 1)))
    wslot = (jnp.cumsum(first.astype(jnp.int32)) - 1) % 2
    later_first = jnp.where(first[None, :] & (blk[None, :] > blk[:, None]), blk[None, :], n_blocks)
    next_pos = jnp.min(later_first, axis=1)
    next_expert = jnp.sum(jnp.where(blk[None, :] == next_pos[:, None], block_expert[None, :], 0), axis=1)
    next_expert = jnp.where(next_pos < n_blocks, next_expert, -1).astype(jnp.int32)

    x_pad = _dispatch(fill_rows, nused, nchunk, chunks, rt, xn2.reshape(T, D), n_rows)
    y_pad = _experts(block_expert, nused, first.astype(jnp.int32), next_expert, wslot.astype(jnp.int32), x_pad,
                     w_gate, b_gate.reshape(N_EXPERTS, 1, D), w_up,
                     b_up.reshape(N_EXPERTS, 1, D), w_down, b_down.reshape(N_EXPERTS, 1, D))
    return _tail, (nchunk, chunks, y_pad, rt, h1.reshape(T, D),
                   p_l.reshape(T, PLE_DIM), row2(norm_ple), w_ple_gate.astype(BF16), w_ple_proj.astype(BF16))


def kernel(x, p, norm_mix, w_in, b_forget, w_attn_out, conv_w, conv_b, conv_ln_g, conv_ln_b, w_conv_out, w_o,
           norm_ffn, w_router, b_router, w_gate, b_gate, w_up, b_up, w_down, b_down, norm_ple, w_ple_gate,
           w_ple_proj, norm_final):
    B, S, D = x.shape
    assert p.shape[0] == 1 and D == D_MODEL
    tail, args = _layer(x, p[0], norm_mix[0], w_in[0], b_forget[0], w_attn_out[0], conv_w[0], conv_b[0],
                        conv_ln_g[0], conv_ln_b[0], w_conv_out[0], w_o[0], norm_ffn[0], w_router[0],
                        b_router[0], w_gate[0], b_gate[0], w_up[0], b_up[0], w_down[0], b_down[0],
                        norm_ple[0], w_ple_gate[0], w_ple_proj[0])
    return tail(*args, norm_final.reshape(1, D)).reshape(B, S, D)
```
